```python
import jax, jax.numpy as jnp
from jax import lax
import numpy as np

D_MODEL = 1024
BATCH = 8
SEQ = 4096
DEPTH = 4

N_HEADS = 8
HEAD_DIM = 64
N_KV = 2
GROUP = N_HEADS // N_KV
ATTN_WIDTH = N_HEADS * HEAD_DIM
KV_WIDTH = N_KV * HEAD_DIM
N_BRANCH = 3
CMP_BLOCK = 32
CMP_STRIDE = 16
CMP_HIDDEN = 256
SEL_BLOCK = 64
N_SEL = 16
WINDOW = 512
Q_BLOCK = 64
FORCE_BONUS = 1e3
CONV_CH = D_MODEL - ATTN_WIDTH
CONV_WIDTH = 31
D_FF = 2816
EPS = 1e-6
SPLITS = [ATTN_WIDTH] + [KV_WIDTH] * 6 + [N_HEADS * N_BRANCH, CONV_CH, CONV_CH]
IN_WIDTH = sum(SPLITS)

kernel_name = "hymba_nsa_conformer_macaron"


def rms_norm(x, g):
    x32 = x.astype(jnp.float32)
    y = x32 * lax.rsqrt(jnp.mean(x32 * x32, axis=-1, keepdims=True) + EPS)
    return (y * g.astype(jnp.float32)).astype(x.dtype)


def layer_norm(x, g, b):
    x32 = x.astype(jnp.float32)
    mu = jnp.mean(x32, axis=-1, keepdims=True)
    var = jnp.mean(jnp.square(x32 - mu), axis=-1, keepdims=True)
    y = (x32 - mu) * lax.rsqrt(var + EPS)
    return (y * g.astype(jnp.float32) + b.astype(jnp.float32)).astype(x.dtype)


def swiglu(h, w_gate, w_up, w_down):
    return (jax.nn.silu(h @ w_gate) * (h @ w_up)) @ w_down


def alibi_slopes():
    return jnp.exp2(-8.0 * jnp.arange(1, N_HEADS + 1, dtype=jnp.float32) / N_HEADS)


def masked_softmax(s, mask):
    p = jax.nn.softmax(jnp.where(mask, s, -1e30), axis=-1)
    return p * mask.astype(p.dtype)


def compress(k, pos, w1, w2):
    B, S = k.shape[0], k.shape[1]
    chunks = k.reshape(B, S // CMP_STRIDE, CMP_STRIDE, N_KV, HEAD_DIM)
    blocks = jnp.concatenate([chunks[:, :-1], chunks[:, 1:]], axis=2)
    blocks = blocks + pos[None, None, :, None, :]
    nc = blocks.shape[1]
    flat = blocks.transpose(0, 1, 3, 2, 4).reshape(B, nc, N_KV, CMP_BLOCK * HEAD_DIM)
    return jax.nn.silu(flat @ w1) @ w2


def nsa_attention(q, k_cmp, v_cmp, k_slc, v_slc, k_win, v_win, gates,
                  pos_k, wk1, wk2, pos_v, wv1, wv2):
    B, S = q.shape[0], q.shape[1]
    scale = HEAD_DIM ** -0.5
    kc = compress(k_cmp, pos_k, wk1, wk2)
    vc = compress(v_cmp, pos_v, wv1, wv2)
    nc = kc.shape[1]
    ns = S // SEL_BLOCK
    n_sel = min(N_SEL, ns)
    cmp_start = jnp.arange(nc) * CMP_STRIDE
    cmp_end = cmp_start + CMP_BLOCK - 1
    sel_start = jnp.arange(ns) * SEL_BLOCK
    overlap = jnp.clip(jnp.minimum(cmp_start[:, None] + CMP_BLOCK, sel_start[None] + SEL_BLOCK)
                       - jnp.maximum(cmp_start[:, None], sel_start[None]), 0).astype(jnp.float32) / CMP_BLOCK
    ks_blocks = k_slc.reshape(B, ns, SEL_BLOCK, N_KV, HEAD_DIM).transpose(0, 3, 1, 2, 4)
    vs_blocks = v_slc.reshape(B, ns, SEL_BLOCK, N_KV, HEAD_DIM).transpose(0, 3, 1, 2, 4)
    pad = jnp.zeros((B, WINDOW, N_KV, HEAD_DIM), k_win.dtype)
    kw_pad = jnp.concatenate([pad, k_win], axis=1)
    vw_pad = jnp.concatenate([pad, v_win], axis=1)
    qg = q.reshape(B, S, N_KV, GROUP, HEAD_DIM)
    gg = jax.nn.sigmoid(gates.reshape(B, S, N_KV, GROUP, N_BRANCH))
    slopes = alibi_slopes().reshape(N_KV, GROUP)[None, :, :, None, None]
    bidx = jnp.arange(B)[:, None, None, None]
    gidx = jnp.arange(N_KV)[None, :, None, None]
    sel_offs = jnp.arange(SEL_BLOCK)
    win_offs = jnp.arange(Q_BLOCK + WINDOW) - WINDOW
    blk = jnp.arange(ns)

    def chunk(c):
        q0 = c * Q_BLOCK
        qc = lax.dynamic_slice_in_dim(qg, q0, Q_BLOCK, axis=1)
        gc = lax.dynamic_slice_in_dim(gg, q0, Q_BLOCK, axis=1)
        t = q0 + jnp.arange(Q_BLOCK)
        m_c = cmp_end[None, :] <= t[:, None]
        d_c = (t[:, None] - cmp_end[None, :]).astype(jnp.float32)
        s_c = jnp.einsum('bqgrd,bngd->bgrqn', qc, kc, preferred_element_type=jnp.float32) * scale - slopes * d_c
        p_c = masked_softmax(s_c, m_c)
        o_cmp = jnp.einsum('bgrqn,bngd->bqgrd', p_c.astype(vc.dtype), vc)
        imp = jnp.einsum('bgrqn,nj->bgqj', p_c, overlap)
        tb = t // SEL_BLOCK
        forced = (blk[None] == 0) | (blk[None] == tb[:, None]) | (blk[None] == tb[:, None] - 1)
        valid = blk[None] * SEL_BLOCK <= t[:, None]
        score = jnp.where(valid, imp + jnp.where(forced, FORCE_BONUS, 0.0), -jnp.inf)
        _, idx = lax.top_k(score, n_sel)
        kg = ks_blocks[bidx, gidx, idx].reshape(B, N_KV, Q_BLOCK, n_sel * SEL_BLOCK, HEAD_DIM)
        vg = vs_blocks[bidx, gidx, idx].reshape(B, N_KV, Q_BLOCK, n_sel * SEL_BLOCK, HEAD_DIM)
        pos = (idx[..., None] * SEL_BLOCK + sel_offs).reshape(B, N_KV, Q_BLOCK, n_sel * SEL_BLOCK)
        d_s = (t[None, None, :, None] - pos)[:, :, None]
        s_s = jnp.einsum('bqgrd,bgqkd->bgrqk', qc, kg, preferred_element_type=jnp.float32) \
            - slopes * d_s.astype(jnp.float32)
        s_s = s_s * 1.0 if False else s_s
        s_s = jnp.einsum('bqgrd,bgqkd->bgrqk', qc, kg, preferred_element_type=jnp.float32) * scale \
            - slopes * d_s.astype(jnp.float32)
        p_s = masked_softmax(s_s, d_s >= 0)
        o_slc = jnp.einsum('bgrqk,bgqkd->bqgrd', p_s.astype(vg.dtype), vg)
        kw = lax.dynamic_slice_in_dim(kw_pad, q0, Q_BLOCK + WINDOW, axis=1)
        vw = lax.dynamic_slice_in_dim(vw_pad, q0, Q_BLOCK + WINDOW, axis=1)
        kpos = q0 + win_offs
        d_w = t[:, None] - kpos[None]
        m_w = (d_w >= 0) & (d_w < WINDOW) & (kpos[None] >= 0)
        s_w = jnp.einsum('bqgrd,bkgd->bgrqk', qc, kw, preferred_element_type=jnp.float32) * scale \
            - slopes * d_w.astype(jnp.float32)
        p_w = masked_softmax(s_w, m_w)
        o_win = jnp.einsum('bgrqk,bkgd->bqgrd', p_w.astype(vw.dtype), vw)
        return gc[..., 0, None] * o_cmp + gc[..., 1, None] * o_slc + gc[..., 2, None] * o_win

    out = lax.map(chunk, jnp.arange(S // Q_BLOCK))
    return out.transpose(1, 0, 2, 3, 4, 5).reshape(B, S, ATTN_WIDTH)


def conv_module(a, b, w, bias, g, beta):
    u = a * jax.nn.sigmoid(b)
    y = lax.conv_general_dilated(u, w[:, None, :].astype(u.dtype), (1,), [(CONV_WIDTH - 1, 0)],
                                 dimension_numbers=('NWC', 'WIO', 'NWC'),
                                 feature_group_count=CONV_CH) + bias
    return jax.nn.silu(layer_norm(y, g, beta))


def setup_inputs(seed: int = 0) -> dict:
    key = jax.random.key(seed)
    ks = jax.random.split(key, 32)
    L, D, F = DEPTH, D_MODEL, D_FF

    def nrm(k, shape, scale):
        return jax.random.normal(k, shape, jnp.float32) * scale

    def gain(k, shape):
        return 1.0 + 0.02 * jax.random.normal(k, shape, jnp.float32)

    return {
        "x": nrm(ks[0], (BATCH, SEQ, D), 1.0),
        "ffn1_norm": gain(ks[1], (L, D)),
        "ffn1_w_gate": nrm(ks[2], (L, D, F), D ** -0.5),
        "ffn1_w_up": nrm(ks[3], (L, D, F), D ** -0.5),
        "ffn1_w_down": nrm(ks[4], (L, F, D), F ** -0.5),
        "mix_norm": gain(ks[5], (L, D)),
        "w_in": nrm(ks[6], (L, D, IN_WIDTH), D ** -0.5),
        "cmp_pos_k": nrm(ks[7], (L, CMP_BLOCK, HEAD_DIM), 0.02),
        "cmp_k_w1": nrm(ks[8], (L, CMP_BLOCK * HEAD_DIM, CMP_HIDDEN), (CMP_BLOCK * HEAD_DIM) ** -0.5),
        "cmp_k_w2": nrm(ks[9], (L, CMP_HIDDEN, HEAD_DIM), CMP_HIDDEN ** -0.5),
        "cmp_pos_v": nrm(ks[10], (L, CMP_BLOCK, HEAD_DIM), 0.02),
        "cmp_v_w1": nrm(ks[11], (L, CMP_BLOCK * HEAD_DIM, CMP_HIDDEN), (CMP_BLOCK * HEAD_DIM) ** -0.5),
        "cmp_v_w2": nrm(ks[12], (L, CMP_HIDDEN, HEAD_DIM), CMP_HIDDEN ** -0.5),
        "conv_w": nrm(ks[13], (L, CONV_WIDTH, CONV_CH), CONV_WIDTH ** -0.5),
        "conv_b": nrm(ks[14], (L, CONV_CH), 0.02),
        "conv_norm_g": gain(ks[15], (L, CONV_CH)),
        "conv_norm_b": nrm(ks[16], (L, CONV_CH), 0.02),
        "w_out": nrm(ks[17], (L, D, D), D ** -0.5),
        "ffn2_norm": gain(ks[18], (L, D)),
        "ffn2_w_gate": nrm(ks[19], (L, D, F), D ** -0.5),
        "ffn2_w_up": nrm(ks[20], (L, D, F), D ** -0.5),
        "ffn2_w_down": nrm(ks[21], (L, F, D), F ** -0.5),
        "final_norm": gain(ks[22], (D,)),
    }


def reference(x, ffn1_norm, ffn1_w_gate, ffn1_w_up, ffn1_w_down, mix_norm, w_in,
              cmp_pos_k, cmp_k_w1, cmp_k_w2, cmp_pos_v, cmp_v_w1, cmp_v_w2,
              conv_w, conv_b, conv_norm_g, conv_norm_b, w_out,
              ffn2_norm, ffn2_w_gate, ffn2_w_up, ffn2_w_down, final_norm):
    B, S = x.shape[0], x.shape[1]
    offsets = [int(v) for v in np.cumsum(SPLITS)[:-1]]
    for l in range(DEPTH):
        x = x + 0.5 * swiglu(rms_norm(x, ffn1_norm[l]), ffn1_w_gate[l], ffn1_w_up[l], ffn1_w_down[l])
        z = rms_norm(x, mix_norm[l]) @ w_in[l]
        q, kc, vc, ks_, vs_, kw, vw, gl, ga, gb = jnp.split(z, offsets, axis=-1)
        kv = lambda t: t.reshape(B, S, N_KV, HEAD_DIM)
        attn = nsa_attention(q, kv(kc), kv(vc), kv(ks_), kv(vs_), kv(kw), kv(vw), gl,
                             cmp_pos_k[l], cmp_k_w1[l], cmp_k_w2[l],
                             cmp_pos_v[l], cmp_v_w1[l], cmp_v_w2[l])
        conv = conv_module(ga, gb, conv_w[l], conv_b[l], conv_norm_g[l], conv_norm_b[l])
        x = x + jnp.concatenate([attn, conv], axis=-1) @ w_out[l]
        x = x + 0.5 * swiglu(rms_norm(x, ffn2_norm[l]), ffn2_w_gate[l], ffn2_w_up[l], ffn2_w_down[l])
    return rms_norm(x, final_norm)
```

```python
import functools

import jax
import jax.numpy as jnp
from jax import lax
from jax.experimental import pallas as pl
from jax.experimental.pallas import tpu as pltpu

F32 = jnp.float32
BF16 = jnp.bfloat16

D_MODEL = 1024
N_HEADS = 8
HEAD_DIM = 64
N_KV = 2
GROUP = N_HEADS // N_KV
ATTN_WIDTH = N_HEADS * HEAD_DIM
KV_WIDTH = N_KV * HEAD_DIM
N_BRANCH = 3
CMP_BLOCK = 32
CMP_STRIDE = 16
CMP_HIDDEN = 256
SEL_BLOCK = 64
N_SEL = 16
WINDOW = 512
Q_BLOCK = 64
FORCE_BONUS = 1e3
CONV_CH = D_MODEL - ATTN_WIDTH
CONV_WIDTH = 31
D_FF = 2816
EPS = 1e-6
NEG = -1e30

V7X_VMEM_BYTES = 64 * 1024 * 1024
VMEM_LIMIT = 56 * 1024 * 1024

GATE_PAD = 128
PROJ_WIDTH = ATTN_WIDTH + 6 * KV_WIDTH + 2 * CONV_CH + GATE_PAD
HALO = 32
ROWS = GROUP * Q_BLOCK
WIN_KEYS = WINDOW + Q_BLOCK


def _const_spec(shape):
    n = len(shape)
    return pl.BlockSpec(shape, lambda *_: (0,) * n, pipeline_mode=pl.Buffered(1))


def _rms(x, g):
    ms = jnp.mean(x * x, axis=-1, keepdims=True)
    return x * lax.rsqrt(ms + EPS) * g


def _ffn_kernel(x_ref, g_ref, wg_ref, wu_ref, wd_ref, fg_ref, o_ref, *, final):
    x = x_ref[...]
    h = _rms(x, g_ref[...]).astype(BF16)
    a = jnp.dot(h, wg_ref[...], preferred_element_type=F32)
    u = jnp.dot(h, wu_ref[...], preferred_element_type=F32)
    act = (a * jax.nn.sigmoid(a) * u).astype(BF16)
    y = jnp.dot(act, wd_ref[...], preferred_element_type=F32)
    o = x + 0.5 * y
    if final:
        o = _rms(o, fg_ref[...])
    o_ref[...] = o


def _ffn(x2, g, wg, wu, wd, fg, *, final, tm):
    n, d = x2.shape
    f = wg.shape[1]
    return pl.pallas_call(
        functools.partial(_ffn_kernel, final=final),
        grid=(n // tm,),
        in_specs=[
            pl.BlockSpec((tm, d), lambda i: (i, 0)),
            _const_spec((1, d)),
            _const_spec((d, f)),
            _const_spec((d, f)),
            _const_spec((f, d)),
            _const_spec((1, d)),
        ],
        out_specs=pl.BlockSpec((tm, d), lambda i: (i, 0)),
        out_shape=jax.ShapeDtypeStruct((n, d), F32),
        compiler_params=pltpu.CompilerParams(
            dimension_semantics=("parallel",), vmem_limit_bytes=VMEM_LIMIT),
        name="ffn",
    )(x2, g, wg, wu, wd, fg)


def _proj_kernel(x_ref, g_ref, w_ref, q_ref, kc_ref, vc_ref, ks_ref, vs_ref, kw_ref, vw_ref,
                 u_ref, gate_ref):
    h = _rms(x_ref[...], g_ref[...]).astype(BF16)
    z = jnp.dot(h, w_ref[...], preferred_element_type=F32)
    o = 0
    q_ref[...] = (z[:, o:o + ATTN_WIDTH] * (HEAD_DIM ** -0.5)).astype(BF16)
    o += ATTN_WIDTH
    kc_ref[...] = z[:, o:o + KV_WIDTH]
    o += KV_WIDTH
    vc_ref[...] = z[:, o:o + KV_WIDTH]
    o += KV_WIDTH
    for ref in (ks_ref, vs_ref, kw_ref, vw_ref):
        ref[...] = z[:, o:o + KV_WIDTH].astype(BF16)
        o += KV_WIDTH
    ga = z[:, o:o + CONV_CH]
    o += CONV_CH
    gb = z[:, o:o + CONV_CH]
    o += CONV_CH
    u_ref[...] = ga * jax.nn.sigmoid(gb)
    gate_ref[...] = jax.nn.sigmoid(z[:, o:o + GATE_PAD])


def _proj(x2, g, w, *, tm):
    n, d = x2.shape
    row = lambda width: pl.BlockSpec((tm, width), lambda i: (i, 0))
    shp = lambda width, dt: jax.ShapeDtypeStruct((n, width), dt)
    return pl.pallas_call(
        _proj_kernel,
        grid=(n // tm,),
        in_specs=[row(d), _const_spec((1, d)), _const_spec((d, PROJ_WIDTH))],
        out_specs=[row(ATTN_WIDTH)] + [row(KV_WIDTH)] * 6 + [row(CONV_CH), row(GATE_PAD)],
        out_shape=[shp(ATTN_WIDTH, BF16), shp(KV_WIDTH, F32), shp(KV_WIDTH, F32)]
        + [shp(KV_WIDTH, BF16)] * 4 + [shp(CONV_CH, F32), shp(GATE_PAD, F32)],
        compiler_params=pltpu.CompilerParams(
            dimension_semantics=("parallel",), vmem_limit_bytes=VMEM_LIMIT),
        name="proj",
    )(x2, g, w)


def _cmp_kernel(kf_ref, vf_ref, pk_ref, w1k_ref, w2k_ref, pv_ref, w1v_ref, w2v_ref, ko_ref, vo_ref):
    nchunk = kf_ref.shape[1]
    row = lax.broadcasted_iota(jnp.int32, (nchunk, 1), 0)
    for f_ref, p_ref, w1_ref, w2_ref, o_ref in ((kf_ref, pk_ref, w1k_ref, w2k_ref, ko_ref),
                                               (vf_ref, pv_ref, w1v_ref, w2v_ref, vo_ref)):
        c = f_ref[0]
        ha = jnp.dot((c + p_ref[0]).astype(BF16), w1_ref[0], preferred_element_type=F32)
        hb = jnp.dot((c + p_ref[1]).astype(BF16), w1_ref[1], preferred_element_type=F32)
        hid = ha + pltpu.roll(hb, nchunk - 1, 0)
        hid = jnp.where(row < nchunk - 1, hid, 0.0)
        act = (hid * jax.nn.sigmoid(hid)).astype(BF16)
        o_ref[0] = jnp.dot(act, w2_ref[...], preferred_element_type=F32).astype(BF16)


def _compress(kf, vf, pk, w1k, w2k, pv, w1v, w2v):
    b, nchunk, width = kf.shape
    blk = pl.BlockSpec((1, nchunk, width), lambda i: (i, 0, 0))
    oblk = pl.BlockSpec((1, nchunk, KV_WIDTH), lambda i: (i, 0, 0))
    wspecs = [_const_spec(pk.shape), _const_spec(w1k.shape), _const_spec(w2k.shape)]
    return pl.pallas_call(
        _cmp_kernel,
        grid=(b,),
        in_specs=[blk, blk] + wspecs + wspecs,
        out_specs=[oblk, oblk],
        out_shape=[jax.ShapeDtypeStruct((b, nchunk, KV_WIDTH), BF16)] * 2,
        compiler_params=pltpu.CompilerParams(
            dimension_semantics=("parallel",), vmem_limit_bytes=VMEM_LIMIT),
        name="compress",
    )(kf, vf, pk, w1k, w2k, pv, w1v, w2v)


def _softmax_rows(s):
    m = jnp.max(s, axis=-1, keepdims=True)
    e = jnp.exp(s - m)
    return e / jnp.sum(e, axis=-1, keepdims=True)


def _attn_kernel(q_ref, kc_ref, vc_ref, ks_ref, vs_ref, kw_ref, vw_ref, gate_ref, o_ref, *, kchunk):
    seq = ks_ref.shape[1]
    ncmp = kc_ref.shape[1]
    nsel = seq // SEL_BLOCK
    c = pl.program_id(1)
    q0 = c * Q_BLOCK
    qf = q_ref[0].astype(F32)
    gates = gate_ref[0]
    rowi = lax.broadcasted_iota(jnp.int32, (ROWS, 1), 0)
    r_row = rowi >> 6
    t_i = q0 + (rowi & (Q_BLOCK - 1))
    t_f = t_i.astype(F32)
    zeros_half = jnp.zeros((ROWS, HEAD_DIM), F32)
    nt = (((1,), (1,)), ((), ()))

    cmp_end = lax.broadcasted_iota(jnp.int32, (1, ncmp), 1) * CMP_STRIDE + (CMP_BLOCK - 1)
    cmp_valid = cmp_end <= t_i
    cmp_end_f = cmp_end.astype(F32)
    n_i = lax.broadcasted_iota(jnp.int32, (ncmp, nsel), 0) * CMP_STRIDE
    j_i = lax.broadcasted_iota(jnp.int32, (ncmp, nsel), 1) * SEL_BLOCK
    ov = jnp.clip(jnp.minimum(n_i + CMP_BLOCK, j_i + SEL_BLOCK) - jnp.maximum(n_i, j_i), 0, None)
    ov = (ov.astype(F32) * (1.0 / CMP_BLOCK)).astype(BF16)
    blk = lax.broadcasted_iota(jnp.int32, (1, nsel), 1)
    forced = (blk == 0) | (blk == c) | (blk == c - 1)
    blk_valid = blk <= c
    w0 = pl.multiple_of(jnp.maximum(q0 - WINDOW, 0), Q_BLOCK)
    wpos = w0 + lax.broadcasted_iota(jnp.int32, (1, WIN_KEYS), 1)
    wd = t_i - wpos
    win_valid = (wd >= 0) & (wd < WINDOW)
    wpos_f = wpos.astype(F32)
    kwin = kw_ref[0, pl.ds(w0, WIN_KEYS), :]
    vwin = vw_ref[0, pl.ds(w0, WIN_KEYS), :]
    kcmp = kc_ref[0]
    vcmp = vc_ref[0]
    n_chunks = (q0 + Q_BLOCK + kchunk - 1) // kchunk

    pieces = [None] * N_HEADS
    for g in range(N_KV):
        slope = jnp.zeros((ROWS, 1), F32)
        for r in range(GROUP):
            slope = jnp.where(r_row == r, 2.0 ** (-(g * GROUP + r + 1)), slope)
        row_term = slope * t_f
        qg = jnp.concatenate(
            [qf[:, (g * GROUP + r) * HEAD_DIM:(g * GROUP + r + 1) * HEAD_DIM] for r in range(GROUP)],
            axis=0)
        qpad = jnp.concatenate([qg, zeros_half] if g == 0 else [zeros_half, qg], axis=1).astype(BF16)

        s = lax.dot_general(qpad, kcmp, nt, preferred_element_type=F32)
        s = jnp.where(cmp_valid, s + (slope * cmp_end_f - row_term), NEG)
        p = jnp.where(cmp_valid, _softmax_rows(s), 0.0)
        o_cmp = jnp.dot(p.astype(BF16), vcmp, preferred_element_type=F32)

        psum = p[0:Q_BLOCK] + p[Q_BLOCK:2 * Q_BLOCK] + p[2 * Q_BLOCK:3 * Q_BLOCK] + p[3 * Q_BLOCK:]
        p_hi = psum.astype(BF16)
        p_lo = (psum - p_hi.astype(F32)).astype(BF16)
        imp = (jnp.dot(p_hi, ov, preferred_element_type=F32)
               + jnp.dot(p_lo, ov, preferred_element_type=F32))
        score = jnp.where(blk_valid, imp + jnp.where(forced, FORCE_BONUS, 0.0), -jnp.inf)
        rank = jnp.zeros((Q_BLOCK, nsel), F32)
        for i in range(nsel):
            col = score[:, i:i + 1]
            later = (blk > i).astype(F32)
            rank = rank + jnp.where(col > score, 1.0, jnp.where(col == score, later, 0.0))
        sel = (rank < float(min(N_SEL, nsel))).astype(BF16)

        def slc_body(kc_i, carry):
            m, l, acc = carry
            k0 = pl.multiple_of(kc_i * kchunk, kchunk)
            kb = ks_ref[0, pl.ds(k0, kchunk), :]
            vb = vs_ref[0, pl.ds(k0, kchunk), :]
            pos = k0 + lax.broadcasted_iota(jnp.int32, (1, kchunk), 1)
            expand = ((pos >> 6) == lax.broadcasted_iota(jnp.int32, (nsel, kchunk), 0))
            selx = jnp.dot(sel, jnp.where(expand, 1.0, 0.0).astype(BF16),
                           preferred_element_type=F32)
            selx = jnp.concatenate([selx] * GROUP, axis=0)
            sc = lax.dot_general(qpad, kb, nt, preferred_element_type=F32)
            sc = sc + (slope * pos.astype(F32) - row_term)
            sc = jnp.where(pos <= t_i, jnp.where(selx > 0.5, sc, NEG), NEG)
            m_new = jnp.maximum(m, jnp.max(sc, axis=-1, keepdims=True))
            alpha = jnp.exp(m - m_new)
            e = jnp.exp(sc - m_new)
            l = alpha * l + jnp.sum(e, axis=-1, keepdims=True)
            acc = alpha * acc + jnp.dot(e.astype(BF16), vb, preferred_element_type=F32)
            return m_new, l, acc

        m0 = jnp.full((ROWS, 1), NEG, F32)
        l0 = jnp.zeros((ROWS, 1), F32)
        a0 = jnp.zeros((ROWS, KV_WIDTH), F32)
        _, l_s, acc_s = lax.fori_loop(0, n_chunks, slc_body, (m0, l0, a0))
        o_slc = acc_s / l_s

        s = lax.dot_general(qpad, kwin, nt, preferred_element_type=F32)
        s = jnp.where(win_valid, s + (slope * wpos_f - row_term), NEG)
        m = jnp.max(s, axis=-1, keepdims=True)
        e = jnp.exp(s - m)
        o_win = jnp.dot(e.astype(BF16), vwin, preferred_element_type=F32) / jnp.sum(e, axis=-1, keepdims=True)

        def gate_col(br):
            return jnp.concatenate(
                [gates[:, (g * GROUP + r) * N_BRANCH + br:(g * GROUP + r) * N_BRANCH + br + 1]
                 for r in range(GROUP)], axis=0)

        og = gate_col(0) * o_cmp + gate_col(1) * o_slc + gate_col(2) * o_win
        for r in range(GROUP):
            pieces[g * GROUP + r] = og[r * Q_BLOCK:(r + 1) * Q_BLOCK, g * HEAD_DIM:(g + 1) * HEAD_DIM]

    o_ref[0] = jnp.concatenate(pieces, axis=1).astype(BF16)


def _attention(q, kc, vc, ks, vs, kw, vw, gates, *, kchunk):
    b, seq, _ = q.shape
    ncmp = kc.shape[1]
    qspec = lambda width: pl.BlockSpec((1, Q_BLOCK, width), lambda i, j: (i, j, 0))
    full = lambda rows: pl.BlockSpec((1, rows, KV_WIDTH), lambda i, j: (i, 0, 0))
    return pl.pallas_call(
        functools.partial(_attn_kernel, kchunk=kchunk),
        grid=(b, seq // Q_BLOCK),
        in_specs=[qspec(ATTN_WIDTH), full(ncmp), full(ncmp), full(seq), full(seq), full(seq), full(seq),
                  qspec(GATE_PAD)],
        out_specs=qspec(ATTN_WIDTH),
        out_shape=jax.ShapeDtypeStruct((b, seq, ATTN_WIDTH), BF16),
        compiler_params=pltpu.CompilerParams(
            dimension_semantics=("parallel", "arbitrary"), vmem_limit_bytes=VMEM_LIMIT),
        name="nsa_attention",
    )(q, kc, vc, ks, vs, kw, vw, gates)


def _mix_out_kernel(x_ref, attn_ref, u_ref, halo_ref, cw_ref, cb_ref, lg_ref, lb_ref, wo_ref, o_ref,
                    ucat_ref, cv_ref, *, rows_per_step):
    ts = u_ref.shape[1]
    i = pl.program_id(1)
    ucat_ref[0:HALO, :] = jnp.where(i > 0, halo_ref[0], 0.0)
    ucat_ref[HALO:HALO + ts, :] = u_ref[0]
    cw = cw_ref[...]
    base = HALO - (CONV_WIDTH - 1)
    for rc in range(ts // rows_per_step):
        r0 = rc * rows_per_step
        acc = jnp.zeros((rows_per_step, CONV_CH), F32) + cb_ref[...]
        for k in range(CONV_WIDTH):
            acc = acc + cw[k:k + 1, :] * ucat_ref[r0 + base + k:r0 + base + k + rows_per_step, :]
        mu = jnp.mean(acc, axis=-1, keepdims=True)
        cen = acc - mu
        var = jnp.mean(cen * cen, axis=-1, keepdims=True)
        y = cen * lax.rsqrt(var + EPS) * lg_ref[...] + lb_ref[...]
        cv_ref[r0:r0 + rows_per_step, :] = (y * jax.nn.sigmoid(y)).astype(BF16)
    heads = jnp.concatenate([attn_ref[0], cv_ref[...]], axis=1)
    o_ref[0] = x_ref[0] + jnp.dot(heads, wo_ref[...], preferred_element_type=F32)


def _mix_out(x, attn, u, cw, cb, lg, lb, wo, *, ts):
    b, seq, d = x.shape
    blk = lambda width: pl.BlockSpec((1, ts, width), lambda bi, i: (bi, i, 0))
    halo = pl.BlockSpec((1, HALO, CONV_CH), lambda bi, i: (bi, jnp.maximum(i * (ts // HALO) - 1, 0), 0))
    return pl.pallas_call(
        functools.partial(_mix_out_kernel, rows_per_step=32),
        grid=(b, seq // ts),
        in_specs=[blk(d), blk(ATTN_WIDTH), blk(CONV_CH), halo,
                  _const_spec(cw.shape), _const_spec(cb.shape), _const_spec(lg.shape), _const_spec(lb.shape),
                  _const_spec(wo.shape)],
        out_specs=blk(d),
        out_shape=jax.ShapeDtypeStruct((b, seq, d), F32),
        scratch_shapes=[pltpu.VMEM((HALO + ts, CONV_CH), F32), pltpu.VMEM((ts, CONV_CH), BF16)],
        compiler_params=pltpu.CompilerParams(
            dimension_semantics=("parallel", "arbitrary"), vmem_limit_bytes=VMEM_LIMIT),
        name="conv_outproj",
    )(x, attn, u, u, cw, cb, lg, lb, wo)


def _prep_w_in(w_in):
    o_gl = ATTN_WIDTH + 6 * KV_WIDTH
    n_gl = N_HEADS * N_BRANCH
    head = w_in[..., :o_gl]
    gl = w_in[..., o_gl:o_gl + n_gl]
    rest = w_in[..., o_gl + n_gl:]
    gl = jnp.pad(gl, ((0, 0), (0, 0), (0, GATE_PAD - n_gl)))
    return jnp.concatenate([head, rest, gl], axis=-1).astype(BF16)


def _prep_cmp(pos, w1, w2):
    nl = pos.shape[0]
    half = CMP_STRIDE
    eye = jnp.eye(N_KV, dtype=w1.dtype)
    w1r = w1.reshape(nl, 2, half, HEAD_DIM, CMP_HIDDEN)
    w1x = w1r[:, :, :, None, :, None, :] * eye[:, None, :, None]
    w1x = w1x.reshape(nl, 2, half * N_KV * HEAD_DIM, N_KV * CMP_HIDDEN).astype(BF16)
    w2x = w2[:, None, :, None, :] * eye[:, None, :, None]
    w2x = w2x.reshape(nl, N_KV * CMP_HIDDEN, N_KV * HEAD_DIM).astype(BF16)
    pr = pos.reshape(nl, 2, half, 1, HEAD_DIM)
    px = jnp.broadcast_to(pr, (nl, 2, half, N_KV, HEAD_DIM)).reshape(nl, 2, 1, half * N_KV * HEAD_DIM)
    return px, w1x, w2x


def kernel(x, ffn1_norm, ffn1_w_gate, ffn1_w_up, ffn1_w_down, mix_norm, w_in, cmp_pos_k, cmp_k_w1, cmp_k_w2,
           cmp_pos_v, cmp_v_w1, cmp_v_w2, conv_w, conv_b, conv_norm_g, conv_norm_b, w_out, ffn2_norm,
           ffn2_w_gate, ffn2_w_up, ffn2_w_down, final_norm):
    b, seq, d = x.shape
    depth = ffn1_norm.shape[0]
    n = b * seq
    assert d == D_MODEL and seq % WINDOW == 0 and seq >= WIN_KEYS
    tm = 512
    ts = 256
    kchunk = 256

    bf = lambda w: w.astype(BF16)
    w1g, w1u, w1d = bf(ffn1_w_gate), bf(ffn1_w_up), bf(ffn1_w_down)
    w2g, w2u, w2d = bf(ffn2_w_gate), bf(ffn2_w_up), bf(ffn2_w_down)
    w_in_p = _prep_w_in(w_in)
    pk, w1k, w2k = _prep_cmp(cmp_pos_k, cmp_k_w1, cmp_k_w2)
    pv, w1v, w2v = _prep_cmp(cmp_pos_v, cmp_v_w1, cmp_v_w2)
    w_o = bf(w_out)
    row = lambda v: v.reshape(1, -1)
    fg = row(final_norm)

    x2 = x.reshape(n, d)
    nchunk = seq // CMP_STRIDE
    for l in range(depth):
        x2 = _ffn(x2, row(ffn1_norm[l]), w1g[l], w1u[l], w1d[l], fg, final=False, tm=tm)
        q, kc, vc, ks, vs, kw, vw, u, gates = _proj(x2, row(mix_norm[l]), w_in_p[l], tm=tm)
        r3 = lambda a: a.reshape(b, seq, a.shape[-1])
        kcc, vcc = _compress(kc.reshape(b, nchunk, CMP_STRIDE * KV_WIDTH),
                             vc.reshape(b, nchunk, CMP_STRIDE * KV_WIDTH),
                             pk[l], w1k[l], w2k[l], pv[l], w1v[l], w2v[l])
        attn = _attention(r3(q), kcc, vcc, r3(ks), r3(vs), r3(kw), r3(vw), r3(gates), kchunk=kchunk)
        x3 = _mix_out(x2.reshape(b, seq, d), attn, r3(u), conv_w[l], row(conv_b[l]), row(conv_norm_g[l]),
                      row(conv_norm_b[l]), w_o[l], ts=ts)
        x2 = _ffn(x3.reshape(n, d), row(ffn2_norm[l]), w2g[l], w2u[l], w2d[l], fg,
                  final=(l == depth - 1), tm=tm)
    return x2.reshape(b, seq, d)
```

```python
import functools

import jax
import jax.numpy as jnp
from jax import lax
from jax.experimental import pallas as pl
from jax.experimental.pallas import tpu as pltpu

F32 = jnp.float32
BF16 = jnp.bfloat16

D_MODEL = 1024
N_HEADS = 8
HEAD_DIM = 64
N_KV = 2
GROUP = N_HEADS // N_KV
ATTN_WIDTH = N_HEADS * HEAD_DIM
KV_WIDTH = N_KV * HEAD_DIM
N_BRANCH = 3
CMP_BLOCK = 32
CMP_STRIDE = 16
CMP_HIDDEN = 256
SEL_BLOCK = 64
N_SEL = 16
WINDOW = 512
Q_BLOCK = 64
FORCE_BONUS = 1e3
CONV_CH = D_MODEL - ATTN_WIDTH
CONV_WIDTH = 31
D_FF = 2816
EPS = 1e-6
NEG = -1e30

V7X_VMEM_BYTES = 64 * 1024 * 1024
VMEM_LIMIT = 56 * 1024 * 1024
LANES = 128
SUBLANES = 8

GATE_PAD = LANES
PROJ_WIDTH = ATTN_WIDTH + 6 * KV_WIDTH + 2 * CONV_CH + GATE_PAD
HALO = 32
ROWS = GROUP * Q_BLOCK
WIN_SPAN = WINDOW + LANES

AUX_BLK = SEL_BLOCK
AUX_OFF = SEL_BLOCK + 1
AUX_ONE_C = SEL_BLOCK + 2
AUX_ONE_Q = SEL_BLOCK + 3
AUX_CMP_N = SEL_BLOCK + 4
AUX_CMP_ONE = SEL_BLOCK + 5
AUX_PAD = SEL_BLOCK + 6


def _const_spec(shape):
    n = len(shape)
    return pl.BlockSpec(shape, lambda *_: (0,) * n, pipeline_mode=pl.Buffered(1))


def _rms(x, g):
    ms = jnp.mean(x * x, axis=-1, keepdims=True)
    return x * lax.rsqrt(ms + EPS) * g


def _ffn_kernel(x_ref, g_ref, wg_ref, wu_ref, wd_ref, fg_ref, o_ref, *, final):
    x = x_ref[...]
    h = _rms(x, g_ref[...]).astype(BF16)
    a = jnp.dot(h, wg_ref[...], preferred_element_type=F32)
    u = jnp.dot(h, wu_ref[...], preferred_element_type=F32)
    act = (a * jax.nn.sigmoid(a) * u).astype(BF16)
    y = jnp.dot(act, wd_ref[...], preferred_element_type=F32)
    o = x + 0.5 * y
    if final:
        o = _rms(o, fg_ref[...])
    o_ref[...] = o


def _ffn(x2, g, wg, wu, wd, fg, *, final, tm):
    n, d = x2.shape
    f = wg.shape[1]
    return pl.pallas_call(
        functools.partial(_ffn_kernel, final=final),
        grid=(n // tm,),
        in_specs=[
            pl.BlockSpec((tm, d), lambda i: (i, 0)),
            _const_spec((1, d)),
            _const_spec((d, f)),
            _const_spec((d, f)),
            _const_spec((f, d)),
            _const_spec((1, d)),
        ],
        out_specs=pl.BlockSpec((tm, d), lambda i: (i, 0)),
        out_shape=jax.ShapeDtypeStruct((n, d), F32),
        compiler_params=pltpu.CompilerParams(
            dimension_semantics=("parallel",), vmem_limit_bytes=VMEM_LIMIT),
        name="ffn",
    )(x2, g, wg, wu, wd, fg)


def _proj_kernel(x_ref, g_ref, w_ref, q_ref, kc_ref, vc_ref, ks_ref, vs_ref, kw_ref, vw_ref,
                 u_ref, gate_ref):
    h = _rms(x_ref[...], g_ref[...]).astype(BF16)
    z = jnp.dot(h, w_ref[...], preferred_element_type=F32)
    o = 0
    q_ref[...] = (z[:, o:o + ATTN_WIDTH] * (HEAD_DIM ** -0.5)).astype(BF16)
    o += ATTN_WIDTH
    kc_ref[...] = z[:, o:o + KV_WIDTH]
    o += KV_WIDTH
    vc_ref[...] = z[:, o:o + KV_WIDTH]
    o += KV_WIDTH
    for ref in (ks_ref, vs_ref, kw_ref, vw_ref):
        ref[...] = z[:, o:o + KV_WIDTH].astype(BF16)
        o += KV_WIDTH
    ga = z[:, o:o + CONV_CH]
    o += CONV_CH
    gb = z[:, o:o + CONV_CH]
    o += CONV_CH
    u_ref[...] = ga * jax.nn.sigmoid(gb)
    gate_ref[...] = jax.nn.sigmoid(z[:, o:o + GATE_PAD])


def _proj(x2, g, w, *, tm):
    n, d = x2.shape
    row = lambda width: pl.BlockSpec((tm, width), lambda i: (i, 0))
    shp = lambda width, dt: jax.ShapeDtypeStruct((n, width), dt)
    return pl.pallas_call(
        _proj_kernel,
        grid=(n // tm,),
        in_specs=[row(d), _const_spec((1, d)), _const_spec((d, PROJ_WIDTH))],
        out_specs=[row(ATTN_WIDTH)] + [row(KV_WIDTH)] * 6 + [row(CONV_CH), row(GATE_PAD)],
        out_shape=[shp(ATTN_WIDTH, BF16), shp(KV_WIDTH, F32), shp(KV_WIDTH, F32)]
        + [shp(KV_WIDTH, BF16)] * 4 + [shp(CONV_CH, F32), shp(GATE_PAD, F32)],
        compiler_params=pltpu.CompilerParams(
            dimension_semantics=("parallel",), vmem_limit_bytes=VMEM_LIMIT),
        name="proj",
    )(x2, g, w)


def _cmp_kernel(kf_ref, vf_ref, pk_ref, w1k_ref, w2k_ref, pv_ref, w1v_ref, w2v_ref, ko_ref, vo_ref):
    nchunk = kf_ref.shape[1]
    row = lax.broadcasted_iota(jnp.int32, (nchunk, 1), 0)
    for f_ref, p_ref, w1_ref, w2_ref, o_ref in ((kf_ref, pk_ref, w1k_ref, w2k_ref, ko_ref),
                                               (vf_ref, pv_ref, w1v_ref, w2v_ref, vo_ref)):
        c = f_ref[0]
        ha = jnp.dot((c + p_ref[0]).astype(BF16), w1_ref[0], preferred_element_type=F32)
        hb = jnp.dot((c + p_ref[1]).astype(BF16), w1_ref[1], preferred_element_type=F32)
        hid = ha + pltpu.roll(hb, nchunk - 1, 0)
        hid = jnp.where(row < nchunk - 1, hid, 0.0)
        act = (hid * jax.nn.sigmoid(hid)).astype(BF16)
        o_ref[0] = jnp.dot(act, w2_ref[...], preferred_element_type=F32).astype(BF16)


def _compress(kf, vf, pk, w1k, w2k, pv, w1v, w2v):
    b, nchunk, width = kf.shape
    blk = pl.BlockSpec((1, nchunk, width), lambda i: (i, 0, 0))
    oblk = pl.BlockSpec((1, nchunk, KV_WIDTH), lambda i: (i, 0, 0))
    wspecs = [_const_spec(pk.shape), _const_spec(w1k.shape), _const_spec(w2k.shape)]
    return pl.pallas_call(
        _cmp_kernel,
        grid=(b,),
        in_specs=[blk, blk] + wspecs + wspecs,
        out_specs=[oblk, oblk],
        out_shape=[jax.ShapeDtypeStruct((b, nchunk, KV_WIDTH), BF16)] * 2,
        compiler_params=pltpu.CompilerParams(
            dimension_semantics=("parallel",), vmem_limit_bytes=VMEM_LIMIT),
        name="compress",
    )(kf, vf, pk, w1k, w2k, pv, w1v, w2v)


def _attn_kernel(q_ref, kc_ref, vct_ref, ks_ref, vst_ref, kw_ref, vwt_ref, gate_ref, aux_ref, ovt_ref,
                 o_ref, *, kchunk):
    ncmp = kc_ref.shape[1]
    nsel = ovt_ref.shape[0]
    c = pl.program_id(1)
    q0 = c * Q_BLOCK
    c_f = c.astype(F32)
    nt = (((1,), (1,)), ((), ()))
    qf = q_ref[0].astype(F32)
    t_row = q0 + (lax.broadcasted_iota(jnp.int32, (1, ROWS), 1) & (Q_BLOCK - 1))
    zeros_half = jnp.zeros((ROWS, HEAD_DIM), F32)
    cmp_end = lax.broadcasted_iota(jnp.int32, (ncmp, 1), 0) * CMP_STRIDE + (CMP_BLOCK - 1)
    cmp_valid = cmp_end <= t_row
    ovt = ovt_ref[...]

    qpad, aux0, qp0, o_cmp, imp = [], [], [], [], []
    for g in range(N_KV):
        qg = jnp.concatenate(
            [qf[:, (g * GROUP + r) * HEAD_DIM:(g * GROUP + r + 1) * HEAD_DIM] for r in range(GROUP)],
            axis=0)
        qpad.append(jnp.concatenate([qg, zeros_half] if g == 0 else [zeros_half, qg], axis=1))
        aux0.append(aux_ref[g, 0] + c_f * aux_ref[g, 1])
        qp0.append(jnp.concatenate([qpad[g], aux0[g]], axis=1).astype(BF16))
        s = lax.dot_general(kc_ref[0], qp0[g], nt, preferred_element_type=F32)
        s = jnp.where(cmp_valid, s, NEG)
        e = jnp.exp(s - jnp.max(s, axis=0, keepdims=True))
        p = jnp.where(cmp_valid, e * (1.0 / jnp.sum(e, axis=0, keepdims=True)), 0.0)
        p_hi = p.astype(BF16)
        p_lo = (p - p_hi.astype(F32)).astype(BF16)
        o_cmp.append(jnp.dot(vct_ref[0], p_hi, preferred_element_type=F32))
        imp_t = (jnp.dot(ovt, p_hi, preferred_element_type=F32)
                 + jnp.dot(ovt, p_lo, preferred_element_type=F32))
        half = imp_t[:, :LANES] + imp_t[:, LANES:]
        imp.append(half + pltpu.roll(half, Q_BLOCK, 1))

    lane = lax.broadcasted_iota(jnp.int32, (1, LANES), 1)
    blk = lax.broadcasted_iota(jnp.int32, (nsel, 1), 0)
    forced = (blk == 0) | (blk == c) | (blk == c - 1)
    blk_valid = blk <= c
    score = jnp.where(lane < Q_BLOCK, imp[0], imp[1])
    score = jnp.where(blk_valid, score + jnp.where(forced, FORCE_BONUS, 0.0), -jnp.inf)
    j_loc = lax.broadcasted_iota(jnp.int32, (SUBLANES, 1), 0)
    nv = nsel // SUBLANES
    tiles = [score[v * SUBLANES:(v + 1) * SUBLANES, :] for v in range(nv)]
    rank = [jnp.zeros((SUBLANES, LANES), F32) for _ in range(nv)]
    for i in range(nsel):
        row = score[i:i + 1, :]
        for v in range(nv):
            if v * SUBLANES + SUBLANES - 1 < i:
                before = jnp.where(row > tiles[v], 1.0, 0.0)
            elif v * SUBLANES > i:
                before = jnp.where(row >= tiles[v], 1.0, 0.0)
            else:
                later = jnp.where(j_loc > i - v * SUBLANES, 1.0, 0.0)
                before = jnp.where(row > tiles[v], 1.0, jnp.where(row == tiles[v], later, 0.0))
            rank[v] = rank[v] + before
    rank = jnp.concatenate(rank, axis=0)
    sel_bias = jnp.where(blk_valid, jnp.where(rank < float(min(N_SEL, nsel)), 0.0, NEG), NEG)
    sel_bias = jnp.concatenate([sel_bias, jnp.zeros((LANES - nsel, LANES), F32)], axis=0).T

    gate_t = jnp.concatenate([gate_ref[0], jnp.zeros((LANES - Q_BLOCK, GATE_PAD), F32)], axis=0).T

    w_al = pl.multiple_of((q0 // LANES) * LANES, LANES)
    kr_top = w_al + lax.broadcasted_iota(jnp.int32, (LANES, 1), 0)
    kr_bot = kr_top + WINDOW
    n_full = q0 // kchunk

    o_win = []
    for g in range(N_KV):
        s = lax.dot_general(kw_ref[0, pl.ds(w_al, WIN_SPAN), :], qp0[g], nt, preferred_element_type=F32)
        s = jnp.concatenate([jnp.where(kr_top > t_row, s[:LANES], NEG),
                             s[LANES:WINDOW],
                             jnp.where(kr_bot <= t_row + WINDOW, s[WINDOW:], NEG)], axis=0)
        e = jnp.exp(s - jnp.max(s, axis=0, keepdims=True))
        o_win.append(jnp.dot(vwt_ref[0, :, pl.ds(w_al, WIN_SPAN)], e.astype(BF16), preferred_element_type=F32)
                     * (1.0 / jnp.sum(e, axis=0, keepdims=True)))

    qp = []
    for g in range(N_KV):
        sb = sel_bias[g * Q_BLOCK:(g + 1) * Q_BLOCK, :]
        aux = aux0[g] + jnp.concatenate([sb] * GROUP, axis=0)
        qp.append(jnp.concatenate([qpad[g], aux], axis=1).astype(BF16))

    def scores(k0):
        kb = ks_ref[0, pl.ds(k0, kchunk), :]
        return tuple(lax.dot_general(kb, qp[g], nt, preferred_element_type=F32) for g in range(N_KV))

    def absorb(k0, s, state):
        vb = vst_ref[0, :, pl.ds(k0, kchunk)]
        out = []
        for g in range(N_KV):
            m, l, acc = state[g]
            m_new = jnp.maximum(m, jnp.max(s[g], axis=0, keepdims=True))
            alpha = jnp.exp(m - m_new)
            e = jnp.exp(s[g] - m_new)
            l = alpha * l + jnp.sum(e, axis=0, keepdims=True)
            acc = alpha * acc + jnp.dot(vb, e.astype(BF16), preferred_element_type=F32)
            out.append((m_new, l, acc))
        return tuple(out)

    def slc_body(i, carry):
        s, state = carry
        k0 = pl.multiple_of(i * kchunk, kchunk)
        s_next = scores(pl.multiple_of(k0 + kchunk, kchunk))
        return s_next, absorb(k0, s, state)

    init = tuple((jnp.full((1, ROWS), NEG, F32), jnp.zeros((1, ROWS), F32), jnp.zeros((KV_WIDTH, ROWS), F32))
                 for _ in range(N_KV))
    s_last, state = lax.fori_loop(0, n_full, slc_body, (scores(0), init))
    k_last = pl.multiple_of(n_full * kchunk, kchunk)
    causal = (k_last + lax.broadcasted_iota(jnp.int32, (kchunk, 1), 0)) <= t_row
    state = absorb(k_last, tuple(jnp.where(causal, s_last[g], NEG) for g in range(N_KV)), state)

    pieces = [None] * N_HEADS
    for g in range(N_KV):
        _, l_s, acc_s = state[g]
        o_slc = acc_s * (1.0 / l_s)

        def gate_row(br):
            return jnp.concatenate(
                [gate_t[(g * GROUP + r) * N_BRANCH + br:(g * GROUP + r) * N_BRANCH + br + 1, :Q_BLOCK]
                 for r in range(GROUP)], axis=1)

        og = (gate_row(0) * o_cmp[g] + gate_row(1) * o_slc + gate_row(2) * o_win[g]).T
        for r in range(GROUP):
            pieces[g * GROUP + r] = og[r * Q_BLOCK:(r + 1) * Q_BLOCK, g * HEAD_DIM:(g + 1) * HEAD_DIM]

    o_ref[0] = jnp.concatenate(pieces, axis=1).astype(BF16)


def _attention(q, kc, vct, ks, vst, kw, vwt, gates, aux, ovt, *, kchunk):
    b, seq, _ = q.shape
    qspec = lambda width: pl.BlockSpec((1, Q_BLOCK, width), lambda i, j: (i, j, 0))
    full = lambda a: pl.BlockSpec((1,) + a.shape[1:], lambda i, j: (i, 0, 0))
    return pl.pallas_call(
        functools.partial(_attn_kernel, kchunk=kchunk),
        grid=(b, seq // Q_BLOCK),
        in_specs=[qspec(ATTN_WIDTH), full(kc), full(vct), full(ks), full(vst), full(kw), full(vwt),
                  qspec(GATE_PAD), _const_spec(aux.shape), _const_spec(ovt.shape)],
        out_specs=qspec(ATTN_WIDTH),
        out_shape=jax.ShapeDtypeStruct((b, seq, ATTN_WIDTH), BF16),
        compiler_params=pltpu.CompilerParams(
            dimension_semantics=("parallel", "arbitrary"), vmem_limit_bytes=VMEM_LIMIT),
        name="nsa_attention",
    )(q, kc, vct, ks, vst, kw, vwt, gates, aux, ovt)


def _attn_tables(seq):
    nsel = seq // SEL_BLOCK
    ncmp = seq // CMP_STRIDE
    lane = jnp.arange(LANES)[None, :]

    def key_consts(pos, is_pad):
        blk = (pos // SEL_BLOCK)[:, None]
        off = (pos % SEL_BLOCK)[:, None]
        live = (~is_pad)[:, None]
        t = jnp.where((lane == blk) & (lane < SEL_BLOCK), 1.0, 0.0)
        t = t + jnp.where(lane == AUX_BLK, blk, 0) + jnp.where(lane == AUX_OFF, off, 0)
        t = t + jnp.where((lane == AUX_ONE_C) | (lane == AUX_ONE_Q), 1.0, 0.0)
        return jnp.where(live, t, jnp.where(lane == AUX_PAD, 1.0, 0.0)).astype(BF16)

    pos = jnp.arange(seq)
    k_slc = key_consts(pos, pos < 0)
    posw = jnp.arange(seq + WINDOW) - WINDOW
    k_win = key_consts(jnp.maximum(posw, 0), posw < 0)
    k_win = jnp.where(lane < SEL_BLOCK, 0, k_win)
    n = jnp.arange(ncmp)[:, None]
    k_cmp = (jnp.where((lane == AUX_ONE_C) | (lane == AUX_ONE_Q) | (lane == AUX_CMP_ONE), 1.0, 0.0)
             + jnp.where(lane == AUX_CMP_N, n, 0)).astype(BF16)

    row = jnp.arange(ROWS)
    qi = (row % Q_BLOCK).astype(F32)[:, None]
    aux = []
    for g in range(N_KV):
        slope = jnp.exp2(-(g * GROUP + row // Q_BLOCK + 1).astype(F32))[:, None]
        t0 = (jnp.where(lane == AUX_BLK, slope * SEL_BLOCK, 0.0) + jnp.where(lane == AUX_OFF, slope, 0.0)
              + jnp.where(lane == AUX_ONE_Q, -slope * qi, 0.0)
              + jnp.where(lane == AUX_CMP_N, slope * CMP_STRIDE, 0.0)
              + jnp.where(lane == AUX_CMP_ONE, slope * (CMP_BLOCK - 1), 0.0)
              + jnp.where(lane == AUX_PAD, NEG, 0.0))
        t1 = jnp.where(lane == AUX_ONE_C, -slope * Q_BLOCK, 0.0)
        aux.append(jnp.stack([t0, t1]))
    aux = jnp.stack(aux).astype(F32)

    cmp_start = jnp.arange(ncmp) * CMP_STRIDE
    sel_start = jnp.arange(nsel) * SEL_BLOCK
    ovt = jnp.clip(jnp.minimum(cmp_start[None] + CMP_BLOCK, sel_start[:, None] + SEL_BLOCK)
                   - jnp.maximum(cmp_start[None], sel_start[:, None]), 0).astype(F32) / CMP_BLOCK
    return k_slc, k_win, k_cmp, aux, ovt.astype(BF16)


def _mix_out_kernel(x_ref, attn_ref, u_ref, halo_ref, cw_ref, cb_ref, lg_ref, lb_ref, wo_ref, o_ref,
                    ucat_ref, cv_ref, *, rows_per_step):
    ts = u_ref.shape[1]
    i = pl.program_id(1)
    ucat_ref[0:HALO, :] = jnp.where(i > 0, halo_ref[0], 0.0)
    ucat_ref[HALO:HALO + ts, :] = u_ref[0]
    cw = cw_ref[...]
    base = HALO - (CONV_WIDTH - 1)
    for rc in range(ts // rows_per_step):
        r0 = rc * rows_per_step
        acc = jnp.zeros((rows_per_step, CONV_CH), F32) + cb_ref[...]
        for k in range(CONV_WIDTH):
            acc = acc + cw[k:k + 1, :] * ucat_ref[r0 + base + k:r0 + base + k + rows_per_step, :]
        mu = jnp.mean(acc, axis=-1, keepdims=True)
        cen = acc - mu
        var = jnp.mean(cen * cen, axis=-1, keepdims=True)
        y = cen * lax.rsqrt(var + EPS) * lg_ref[...] + lb_ref[...]
        cv_ref[r0:r0 + rows_per_step, :] = (y * jax.nn.sigmoid(y)).astype(BF16)
    heads = jnp.concatenate([attn_ref[0], cv_ref[...]], axis=1)
    o_ref[0] = x_ref[0] + jnp.dot(heads, wo_ref[...], preferred_element_type=F32)


def _mix_out(x, attn, u, cw, cb, lg, lb, wo, *, ts):
    b, seq, d = x.shape
    blk = lambda width: pl.BlockSpec((1, ts, width), lambda bi, i: (bi, i, 0))
    halo = pl.BlockSpec((1, HALO, CONV_CH), lambda bi, i: (bi, jnp.maximum(i * (ts // HALO) - 1, 0), 0))
    return pl.pallas_call(
        functools.partial(_mix_out_kernel, rows_per_step=32),
        grid=(b, seq // ts),
        in_specs=[blk(d), blk(ATTN_WIDTH), blk(CONV_CH), halo,
                  _const_spec(cw.shape), _const_spec(cb.shape), _const_spec(lg.shape), _const_spec(lb.shape),
                  _const_spec(wo.shape)],
        out_specs=blk(d),
        out_shape=jax.ShapeDtypeStruct((b, seq, d), F32),
        scratch_shapes=[pltpu.VMEM((HALO + ts, CONV_CH), F32), pltpu.VMEM((ts, CONV_CH), BF16)],
        compiler_params=pltpu.CompilerParams(
            dimension_semantics=("parallel", "arbitrary"), vmem_limit_bytes=VMEM_LIMIT),
        name="conv_outproj",
    )(x, attn, u, u, cw, cb, lg, lb, wo)


def _prep_w_in(w_in):
    o_gl = ATTN_WIDTH + 6 * KV_WIDTH
    n_gl = N_HEADS * N_BRANCH
    head = w_in[..., :o_gl]
    gl = w_in[..., o_gl:o_gl + n_gl]
    rest = w_in[..., o_gl + n_gl:]
    gl = jnp.pad(gl, ((0, 0), (0, 0), (0, GATE_PAD - n_gl)))
    return jnp.concatenate([head, rest, gl], axis=-1).astype(BF16)


def _prep_cmp(pos, w1, w2):
    nl = pos.shape[0]
    half = CMP_STRIDE
    eye = jnp.eye(N_KV, dtype=w1.dtype)
    w1r = w1.reshape(nl, 2, half, HEAD_DIM, CMP_HIDDEN)
    w1x = w1r[:, :, :, None, :, None, :] * eye[:, None, :, None]
    w1x = w1x.reshape(nl, 2, half * N_KV * HEAD_DIM, N_KV * CMP_HIDDEN).astype(BF16)
    w2x = w2[:, None, :, None, :] * eye[:, None, :, None]
    w2x = w2x.reshape(nl, N_KV * CMP_HIDDEN, N_KV * HEAD_DIM).astype(BF16)
    pr = pos.reshape(nl, 2, half, 1, HEAD_DIM)
    px = jnp.broadcast_to(pr, (nl, 2, half, N_KV, HEAD_DIM)).reshape(nl, 2, 1, half * N_KV * HEAD_DIM)
    return px, w1x, w2x


def kernel(x, ffn1_norm, ffn1_w_gate, ffn1_w_up, ffn1_w_down, mix_norm, w_in, cmp_pos_k, cmp_k_w1, cmp_k_w2,
           cmp_pos_v, cmp_v_w1, cmp_v_w2, conv_w, conv_b, conv_norm_g, conv_norm_b, w_out, ffn2_norm,
           ffn2_w_gate, ffn2_w_up, ffn2_w_down, final_norm):
    b, seq, d = x.shape
    depth = ffn1_norm.shape[0]
    n = b * seq
    tm = 512
    ts = 256
    kchunk = 256
    assert d == D_MODEL and seq % kchunk == 0 and seq % tm == 0 and seq // SEL_BLOCK <= SEL_BLOCK

    bf = lambda w: w.astype(BF16)
    w1g, w1u, w1d = bf(ffn1_w_gate), bf(ffn1_w_up), bf(ffn1_w_down)
    w2g, w2u, w2d = bf(ffn2_w_gate), bf(ffn2_w_up), bf(ffn2_w_down)
    w_in_p = _prep_w_in(w_in)
    pk, w1k, w2k = _prep_cmp(cmp_pos_k, cmp_k_w1, cmp_k_w2)
    pv, w1v, w2v = _prep_cmp(cmp_pos_v, cmp_v_w1, cmp_v_w2)
    w_o = bf(w_out)
    row = lambda v: v.reshape(1, -1)
    fg = row(final_norm)
    kconst_slc, kconst_win, kconst_cmp, aux, ovt = _attn_tables(seq)

    def with_consts(k, consts):
        return jnp.concatenate([k, jnp.broadcast_to(consts, (b,) + consts.shape)], axis=-1)

    x2 = x.reshape(n, d)
    nchunk = seq // CMP_STRIDE
    r3 = lambda a: a.reshape(b, seq, a.shape[-1])
    tr = lambda a: jnp.swapaxes(a, 1, 2)
    lpad = lambda a: jnp.pad(a, ((0, 0), (WINDOW, 0), (0, 0)))
    for l in range(depth):
        x2 = _ffn(x2, row(ffn1_norm[l]), w1g[l], w1u[l], w1d[l], fg, final=False, tm=tm)
        q, kc, vc, ks, vs, kw, vw, u, gates = _proj(x2, row(mix_norm[l]), w_in_p[l], tm=tm)
        kcc, vcc = _compress(kc.reshape(b, nchunk, CMP_STRIDE * KV_WIDTH),
                             vc.reshape(b, nchunk, CMP_STRIDE * KV_WIDTH),
                             pk[l], w1k[l], w2k[l], pv[l], w1v[l], w2v[l])
        attn = _attention(r3(q), with_consts(kcc, kconst_cmp), tr(vcc),
                          with_consts(r3(ks), kconst_slc), tr(r3(vs)),
                          with_consts(lpad(r3(kw)), kconst_win), tr(lpad(r3(vw))),
                          r3(gates), aux, ovt, kchunk=kchunk)
        x3 = _mix_out(x2.reshape(b, seq, d), attn, r3(u), conv_w[l], row(conv_b[l]), row(conv_norm_g[l]),
                      row(conv_norm_b[l]), w_o[l], ts=ts)
        x2 = _ffn(x3.reshape(n, d), row(ffn2_norm[l]), w2g[l], w2u[l], w2d[l], fg,
                  final=(l == depth - 1), tm=tm)
    return x2.reshape(b, seq, d)
```

```python
import functools

import jax
import jax.numpy as jnp
from jax import lax
from jax.experimental import pallas as pl
from jax.experimental.pallas import tpu as pltpu

F32 = jnp.float32
BF16 = jnp.bfloat16

D_MODEL = 1024
N_HEADS = 8
HEAD_DIM = 64
N_KV = 2
GROUP = N_HEADS // N_KV
ATTN_WIDTH = N_HEADS * HEAD_DIM
KV_WIDTH = N_KV * HEAD_DIM
N_BRANCH = 3
CMP_BLOCK = 32
CMP_STRIDE = 16
CMP_HIDDEN = 256
SEL_BLOCK = 64
N_SEL = 16
WINDOW = 512
Q_BLOCK = 64
FORCE_BONUS = 1e3
CONV_CH = D_MODEL - ATTN_WIDTH
CONV_WIDTH = 31
D_FF = 2816
EPS = 1e-6
NEG = -1e30

V7X_VMEM_BYTES = 64 * 1024 * 1024
VMEM_LIMIT = 56 * 1024 * 1024
LANES = 128
SUBLANES = 8

GATE_PAD = LANES
PROJ_WIDTH = ATTN_WIDTH + 6 * KV_WIDTH + 2 * CONV_CH + GATE_PAD
HALO = 32
ROWS = GROUP * Q_BLOCK
WIN_SPAN = WINDOW + LANES
STRIP = 64
ONES_ROWS = 16
VT_ROWS = KV_WIDTH + ONES_ROWS

AUX_BLK = SEL_BLOCK
AUX_OFF = SEL_BLOCK + 1
AUX_ONE_C = SEL_BLOCK + 2
AUX_ONE_Q = SEL_BLOCK + 3
AUX_CMP_N = SEL_BLOCK + 4
AUX_CMP_ONE = SEL_BLOCK + 5
AUX_PAD = SEL_BLOCK + 6


def _const_spec(shape):
    n = len(shape)
    return pl.BlockSpec(shape, lambda *_: (0,) * n, pipeline_mode=pl.Buffered(1))


def _rms(x, g):
    ms = jnp.mean(x * x, axis=-1, keepdims=True)
    return x * lax.rsqrt(ms + EPS) * g


def _ffn_kernel(x_ref, g_ref, wg_ref, wu_ref, wd_ref, fg_ref, o_ref, *, final):
    x = x_ref[...]
    h = _rms(x, g_ref[...]).astype(BF16)
    a = jnp.dot(h, wg_ref[...], preferred_element_type=F32)
    u = jnp.dot(h, wu_ref[...], preferred_element_type=F32)
    act = (a * jax.nn.sigmoid(a) * u).astype(BF16)
    y = jnp.dot(act, wd_ref[...], preferred_element_type=F32)
    o = x + 0.5 * y
    if final:
        o = _rms(o, fg_ref[...])
    o_ref[...] = o


def _ffn(x2, g, wg, wu, wd, fg, *, final, tm):
    n, d = x2.shape
    f = wg.shape[1]
    return pl.pallas_call(
        functools.partial(_ffn_kernel, final=final),
        grid=(n // tm,),
        in_specs=[
            pl.BlockSpec((tm, d), lambda i: (i, 0)),
            _const_spec((1, d)),
            _const_spec((d, f)),
            _const_spec((d, f)),
            _const_spec((f, d)),
            _const_spec((1, d)),
        ],
        out_specs=pl.BlockSpec((tm, d), lambda i: (i, 0)),
        out_shape=jax.ShapeDtypeStruct((n, d), F32),
        compiler_params=pltpu.CompilerParams(
            dimension_semantics=("parallel",), vmem_limit_bytes=VMEM_LIMIT),
        name="ffn",
    )(x2, g, wg, wu, wd, fg)


def _proj_kernel(x_ref, g_ref, w_ref, q_ref, kc_ref, vc_ref, ks_ref, vs_ref, kw_ref, vw_ref,
                 u_ref, gate_ref):
    h = _rms(x_ref[...], g_ref[...]).astype(BF16)
    z = jnp.dot(h, w_ref[...], preferred_element_type=F32)
    o = 0
    q_ref[...] = (z[:, o:o + ATTN_WIDTH] * (HEAD_DIM ** -0.5)).astype(BF16)
    o += ATTN_WIDTH
    kc_ref[...] = z[:, o:o + KV_WIDTH]
    o += KV_WIDTH
    vc_ref[...] = z[:, o:o + KV_WIDTH]
    o += KV_WIDTH
    for ref in (ks_ref, vs_ref, kw_ref, vw_ref):
        ref[...] = z[:, o:o + KV_WIDTH].astype(BF16)
        o += KV_WIDTH
    ga = z[:, o:o + CONV_CH]
    o += CONV_CH
    gb = z[:, o:o + CONV_CH]
    o += CONV_CH
    u_ref[...] = ga * jax.nn.sigmoid(gb)
    gate_ref[...] = jax.nn.sigmoid(z[:, o:o + GATE_PAD])


def _proj(x2, g, w, *, tm):
    n, d = x2.shape
    row = lambda width: pl.BlockSpec((tm, width), lambda i: (i, 0))
    shp = lambda width, dt: jax.ShapeDtypeStruct((n, width), dt)
    return pl.pallas_call(
        _proj_kernel,
        grid=(n // tm,),
        in_specs=[row(d), _const_spec((1, d)), _const_spec((d, PROJ_WIDTH))],
        out_specs=[row(ATTN_WIDTH)] + [row(KV_WIDTH)] * 6 + [row(CONV_CH), row(GATE_PAD)],
        out_shape=[shp(ATTN_WIDTH, BF16), shp(KV_WIDTH, F32), shp(KV_WIDTH, F32)]
        + [shp(KV_WIDTH, BF16)] * 4 + [shp(CONV_CH, F32), shp(GATE_PAD, F32)],
        compiler_params=pltpu.CompilerParams(
            dimension_semantics=("parallel",), vmem_limit_bytes=VMEM_LIMIT),
        name="proj",
    )(x2, g, w)


def _cmp_kernel(kf_ref, vf_ref, pk_ref, w1k_ref, w2k_ref, pv_ref, w1v_ref, w2v_ref, ko_ref, vo_ref):
    nchunk = kf_ref.shape[1]
    row = lax.broadcasted_iota(jnp.int32, (nchunk, 1), 0)
    for f_ref, p_ref, w1_ref, w2_ref, o_ref in ((kf_ref, pk_ref, w1k_ref, w2k_ref, ko_ref),
                                               (vf_ref, pv_ref, w1v_ref, w2v_ref, vo_ref)):
        c = f_ref[0]
        ha = jnp.dot((c + p_ref[0]).astype(BF16), w1_ref[0], preferred_element_type=F32)
        hb = jnp.dot((c + p_ref[1]).astype(BF16), w1_ref[1], preferred_element_type=F32)
        hid = ha + pltpu.roll(hb, nchunk - 1, 0)
        hid = jnp.where(row < nchunk - 1, hid, 0.0)
        act = (hid * jax.nn.sigmoid(hid)).astype(BF16)
        o_ref[0] = jnp.dot(act, w2_ref[...], preferred_element_type=F32).astype(BF16)


def _compress(kf, vf, pk, w1k, w2k, pv, w1v, w2v):
    b, nchunk, width = kf.shape
    blk = pl.BlockSpec((1, nchunk, width), lambda i: (i, 0, 0))
    oblk = pl.BlockSpec((1, nchunk, KV_WIDTH), lambda i: (i, 0, 0))
    wspecs = [_const_spec(pk.shape), _const_spec(w1k.shape), _const_spec(w2k.shape)]
    return pl.pallas_call(
        _cmp_kernel,
        grid=(b,),
        in_specs=[blk, blk] + wspecs + wspecs,
        out_specs=[oblk, oblk],
        out_shape=[jax.ShapeDtypeStruct((b, nchunk, KV_WIDTH), BF16)] * 2,
        compiler_params=pltpu.CompilerParams(
            dimension_semantics=("parallel",), vmem_limit_bytes=VMEM_LIMIT),
        name="compress",
    )(kf, vf, pk, w1k, w2k, pv, w1v, w2v)


def _attn_kernel(q_ref, kc_ref, vct_ref, ks_ref, vst_ref, kw_ref, vwt_ref, gate_ref, aux_ref, ovt_ref,
                 o_ref, s0_scr, s1_scr, e0_scr, e1_scr, acc_scr, *, kchunk):
    ncmp = kc_ref.shape[1]
    nsel = ovt_ref.shape[0] - ONES_ROWS
    c = pl.program_id(1)
    q0 = c * Q_BLOCK
    c_f = c.astype(F32)
    nt = (((1,), (1,)), ((), ()))
    qf = q_ref[0].astype(F32)
    t_row = q0 + (lax.broadcasted_iota(jnp.int32, (1, ROWS), 1) & (Q_BLOCK - 1))
    zeros_half = jnp.zeros((ROWS, HEAD_DIM), F32)
    cmp_end = lax.broadcasted_iota(jnp.int32, (ncmp, 1), 0) * CMP_STRIDE + (CMP_BLOCK - 1)
    cmp_valid = cmp_end <= t_row
    has_cmp = t_row >= CMP_BLOCK - 1
    ovt = ovt_ref[...]

    qpad, aux0, qp0, o_cmp, imp = [], [], [], [], []
    for g in range(N_KV):
        qg = jnp.concatenate(
            [qf[:, (g * GROUP + r) * HEAD_DIM:(g * GROUP + r + 1) * HEAD_DIM] for r in range(GROUP)],
            axis=0)
        qpad.append(jnp.concatenate([qg, zeros_half] if g == 0 else [zeros_half, qg], axis=1))
        aux0.append(aux_ref[g, 0] + c_f * aux_ref[g, 1])
        qp0.append(jnp.concatenate([qpad[g], aux0[g]], axis=1).astype(BF16))
        s = lax.dot_general(kc_ref[0], qp0[g], nt, preferred_element_type=F32)
        s = jnp.where(cmp_valid, s, NEG)
        e = jnp.exp(s - jnp.max(s, axis=0, keepdims=True))
        e_hi = e.astype(BF16)
        e_lo = (e - e_hi.astype(F32)).astype(BF16)
        red = (jnp.dot(ovt, e_hi, preferred_element_type=F32)
               + jnp.dot(ovt, e_lo, preferred_element_type=F32))
        rinv = jnp.where(has_cmp, 1.0 / red[nsel:nsel + 1], 0.0)
        o_cmp.append(jnp.dot(vct_ref[0], e_hi, preferred_element_type=F32) * rinv)
        imp_t = red[:nsel] * rinv
        half = imp_t[:, :LANES] + imp_t[:, LANES:]
        imp.append(half + pltpu.roll(half, Q_BLOCK, 1))

    lane = lax.broadcasted_iota(jnp.int32, (1, LANES), 1)
    blk = lax.broadcasted_iota(jnp.int32, (nsel, 1), 0)
    forced = (blk == 0) | (blk == c) | (blk == c - 1)
    blk_valid = blk <= c
    score = jnp.where(lane < Q_BLOCK, imp[0], imp[1])
    score = jnp.where(blk_valid, score + jnp.where(forced, FORCE_BONUS, 0.0), -jnp.inf)
    j_loc = lax.broadcasted_iota(jnp.int32, (SUBLANES, 1), 0)
    nv = nsel // SUBLANES
    tiles = [score[v * SUBLANES:(v + 1) * SUBLANES, :] for v in range(nv)]
    rank = [jnp.zeros((SUBLANES, LANES), F32) for _ in range(nv)]
    for i in range(nsel):
        row = score[i:i + 1, :]
        for v in range(nv):
            if v * SUBLANES + SUBLANES - 1 < i:
                before = jnp.where(row > tiles[v], 1.0, 0.0)
            elif v * SUBLANES > i:
                before = jnp.where(row >= tiles[v], 1.0, 0.0)
            else:
                later = jnp.where(j_loc > i - v * SUBLANES, 1.0, 0.0)
                before = jnp.where(row > tiles[v], 1.0, jnp.where(row == tiles[v], later, 0.0))
            rank[v] = rank[v] + before
    rank = jnp.concatenate(rank, axis=0)
    sel_bias = jnp.where(blk_valid, jnp.where(rank < float(min(N_SEL, nsel)), 0.0, NEG), NEG)
    sel_bias = jnp.concatenate([sel_bias, jnp.zeros((LANES - nsel, LANES), F32)], axis=0).T

    gate_t = jnp.concatenate([gate_ref[0], jnp.zeros((LANES - Q_BLOCK, GATE_PAD), F32)], axis=0).T

    w_al = pl.multiple_of((q0 // LANES) * LANES, LANES)
    kr_top = w_al + lax.broadcasted_iota(jnp.int32, (LANES, 1), 0)
    kr_bot = kr_top + WINDOW
    n_full = q0 // kchunk

    o_win = []
    for g in range(N_KV):
        s = lax.dot_general(kw_ref[0, pl.ds(w_al, WIN_SPAN), :], qp0[g], nt, preferred_element_type=F32)
        s = jnp.concatenate([jnp.where(kr_top > t_row, s[:LANES], NEG),
                             s[LANES:WINDOW],
                             jnp.where(kr_bot <= t_row + WINDOW, s[WINDOW:], NEG)], axis=0)
        e = jnp.exp(s - jnp.max(s, axis=0, keepdims=True))
        res = jnp.dot(vwt_ref[0, :, pl.ds(w_al, WIN_SPAN)], e.astype(BF16), preferred_element_type=F32)
        o_win.append(res[:KV_WIDTH] * (1.0 / res[KV_WIDTH:KV_WIDTH + 1]))

    qp = []
    for g in range(N_KV):
        sb = sel_bias[g * Q_BLOCK:(g + 1) * Q_BLOCK, :]
        aux = aux0[g] + jnp.concatenate([sb] * GROUP, axis=0)
        qp.append(jnp.concatenate([qpad[g], aux], axis=1).astype(BF16))
    s_bufs = (s0_scr, s1_scr)
    e_bufs = (e0_scr, e1_scr)
    n_strip = kchunk // STRIP

    def produce(k0, s_buf, g):
        s = lax.dot_general(ks_ref[0, pl.ds(k0, kchunk), :], qp[g], nt, preferred_element_type=F32)
        s_buf[g] = s
        return jnp.max(s, axis=0, keepdims=True)

    def soften(s_buf, e_buf, smax, m, g):
        m_new = jnp.maximum(m, smax)
        for j in range(n_strip):
            rows = slice(j * STRIP, (j + 1) * STRIP)
            e_buf[g, rows, :] = jnp.exp(s_buf[g, rows, :] - m_new).astype(BF16)
        return m_new, jnp.exp(m - m_new)

    def add_values(k0, e_buf, alpha, g):
        vb = vst_ref[0, :, pl.ds(k0, kchunk)]
        acc_scr[g] = alpha * acc_scr[g] + jnp.dot(vb, e_buf[g], preferred_element_type=F32)

    def step(j, carry, cur, nxt):
        smax, m, alpha_prev = carry
        k0 = pl.multiple_of(j * kchunk, kchunk)
        k_prev = pl.multiple_of(jnp.maximum(k0 - kchunk, 0), kchunk)
        k_next = pl.multiple_of(k0 + kchunk, kchunk)
        smax_n, m_n, alpha_n = [], [], []
        for g in range(N_KV):
            smax_n.append(produce(k_next, s_bufs[nxt], g))
            m_g, alpha_g = soften(s_bufs[cur], e_bufs[cur], smax[g], m[g], g)
            m_n.append(m_g)
            alpha_n.append(alpha_g)
            add_values(k_prev, e_bufs[nxt], alpha_prev[g], g)
        return tuple(smax_n), tuple(m_n), tuple(alpha_n)

    def start(cur, nxt):
        e_bufs[nxt][...] = jnp.zeros(e_bufs[nxt].shape, BF16)
        return tuple(produce(0, s_bufs[cur], g) for g in range(N_KV))

    def slc_body(j, carry):
        return lax.cond(((n_full - j) & 1) == 0,
                        lambda cr: step(j, cr, 0, 1), lambda cr: step(j, cr, 1, 0), carry)

    acc_scr[...] = jnp.zeros(acc_scr.shape, F32)
    smax0 = lax.cond((n_full & 1) == 0, lambda: start(0, 1), lambda: start(1, 0))
    m0 = tuple(jnp.full((1, ROWS), NEG, F32) for _ in range(N_KV))
    one = tuple(jnp.ones((1, ROWS), F32) for _ in range(N_KV))
    _, m_run, alpha_prev = lax.fori_loop(0, n_full, slc_body, (smax0, m0, one))
    k_last = pl.multiple_of(n_full * kchunk, kchunk)
    k_prev = pl.multiple_of(jnp.maximum(k_last - kchunk, 0), kchunk)
    diag = pl.multiple_of(q0 - k_last, Q_BLOCK)
    causal = (q0 + lax.broadcasted_iota(jnp.int32, (Q_BLOCK, 1), 0)) <= t_row
    for g in range(N_KV):
        add_values(k_prev, e1_scr, alpha_prev[g], g)
        s0_scr[g, pl.ds(diag, Q_BLOCK), :] = jnp.where(causal, s0_scr[g, pl.ds(diag, Q_BLOCK), :], NEG)
        mx = s0_scr[g, 0:STRIP, :]
        for j in range(1, n_strip):
            mx = jnp.maximum(mx, s0_scr[g, j * STRIP:(j + 1) * STRIP, :])
        _, alpha_last = soften(s0_scr, e0_scr, jnp.max(mx, axis=0, keepdims=True), m_run[g], g)
        add_values(k_last, e0_scr, alpha_last, g)

    pieces = [None] * N_HEADS
    for g in range(N_KV):
        o_slc = acc_scr[g, :KV_WIDTH, :] * (1.0 / acc_scr[g, KV_WIDTH:KV_WIDTH + 1, :])

        def gate_row(br):
            return jnp.concatenate(
                [gate_t[(g * GROUP + r) * N_BRANCH + br:(g * GROUP + r) * N_BRANCH + br + 1, :Q_BLOCK]
                 for r in range(GROUP)], axis=1)

        og = (gate_row(0) * o_cmp[g] + gate_row(1) * o_slc + gate_row(2) * o_win[g]).T
        for r in range(GROUP):
            pieces[g * GROUP + r] = og[r * Q_BLOCK:(r + 1) * Q_BLOCK, g * HEAD_DIM:(g + 1) * HEAD_DIM]

    o_ref[0] = jnp.concatenate(pieces, axis=1).astype(BF16)


def _attention(q, kc, vct, ks, vst, kw, vwt, gates, aux, ovt, *, kchunk):
    b, seq, _ = q.shape
    qspec = lambda width: pl.BlockSpec((1, Q_BLOCK, width), lambda i, j: (i, j, 0))
    full = lambda a: pl.BlockSpec((1,) + a.shape[1:], lambda i, j: (i, 0, 0))
    return pl.pallas_call(
        functools.partial(_attn_kernel, kchunk=kchunk),
        grid=(b, seq // Q_BLOCK),
        in_specs=[qspec(ATTN_WIDTH), full(kc), full(vct), full(ks), full(vst), full(kw), full(vwt),
                  qspec(GATE_PAD), _const_spec(aux.shape), _const_spec(ovt.shape)],
        out_specs=qspec(ATTN_WIDTH),
        out_shape=jax.ShapeDtypeStruct((b, seq, ATTN_WIDTH), BF16),
        scratch_shapes=[pltpu.VMEM((N_KV, kchunk, ROWS), F32), pltpu.VMEM((N_KV, kchunk, ROWS), F32),
                        pltpu.VMEM((N_KV, kchunk, ROWS), BF16), pltpu.VMEM((N_KV, kchunk, ROWS), BF16),
                        pltpu.VMEM((N_KV, VT_ROWS, ROWS), F32)],
        compiler_params=pltpu.CompilerParams(
            dimension_semantics=("parallel", "arbitrary"), vmem_limit_bytes=VMEM_LIMIT),
        name="nsa_attention",
    )(q, kc, vct, ks, vst, kw, vwt, gates, aux, ovt)


def _attn_tables(seq):
    nsel = seq // SEL_BLOCK
    ncmp = seq // CMP_STRIDE
    lane = jnp.arange(LANES)[None, :]

    def key_consts(pos, is_pad):
        blk = (pos // SEL_BLOCK)[:, None]
        off = (pos % SEL_BLOCK)[:, None]
        live = (~is_pad)[:, None]
        t = jnp.where((lane == blk) & (lane < SEL_BLOCK), 1.0, 0.0)
        t = t + jnp.where(lane == AUX_BLK, blk, 0) + jnp.where(lane == AUX_OFF, off, 0)
        t = t + jnp.where((lane == AUX_ONE_C) | (lane == AUX_ONE_Q), 1.0, 0.0)
        return jnp.where(live, t, jnp.where(lane == AUX_PAD, 1.0, 0.0)).astype(BF16)

    pos = jnp.arange(seq)
    k_slc = key_consts(pos, pos < 0)
    posw = jnp.arange(seq + WINDOW) - WINDOW
    k_win = key_consts(jnp.maximum(posw, 0), posw < 0)
    k_win = jnp.where(lane < SEL_BLOCK, 0, k_win)
    n = jnp.arange(ncmp)[:, None]
    k_cmp = (jnp.where((lane == AUX_ONE_C) | (lane == AUX_ONE_Q) | (lane == AUX_CMP_ONE), 1.0, 0.0)
             + jnp.where(lane == AUX_CMP_N, n, 0)).astype(BF16)

    row = jnp.arange(ROWS)
    qi = (row % Q_BLOCK).astype(F32)[:, None]
    aux = []
    for g in range(N_KV):
        slope = jnp.exp2(-(g * GROUP + row // Q_BLOCK + 1).astype(F32))[:, None]
        t0 = (jnp.where(lane == AUX_BLK, slope * SEL_BLOCK, 0.0) + jnp.where(lane == AUX_OFF, slope, 0.0)
              + jnp.where(lane == AUX_ONE_Q, -slope * qi, 0.0)
              + jnp.where(lane == AUX_CMP_N, slope * CMP_STRIDE, 0.0)
              + jnp.where(lane == AUX_CMP_ONE, slope * (CMP_BLOCK - 1), 0.0)
              + jnp.where(lane == AUX_PAD, NEG, 0.0))
        t1 = jnp.where(lane == AUX_ONE_C, -slope * Q_BLOCK, 0.0)
        aux.append(jnp.stack([t0, t1]))
    aux = jnp.stack(aux).astype(F32)

    cmp_start = jnp.arange(ncmp) * CMP_STRIDE
    sel_start = jnp.arange(nsel) * SEL_BLOCK
    ovt = jnp.clip(jnp.minimum(cmp_start[None] + CMP_BLOCK, sel_start[:, None] + SEL_BLOCK)
                   - jnp.maximum(cmp_start[None], sel_start[:, None]), 0).astype(F32) / CMP_BLOCK
    ovt = jnp.concatenate([ovt, jnp.ones((ONES_ROWS, ncmp), F32)], axis=0)
    return k_slc, k_win, k_cmp, aux, ovt.astype(BF16)


def _mix_out_kernel(x_ref, attn_ref, u_ref, halo_ref, cw_ref, cb_ref, lg_ref, lb_ref, wo_ref, o_ref,
                    ucat_ref, cv_ref, *, rows_per_step):
    ts = u_ref.shape[1]
    i = pl.program_id(1)
    ucat_ref[0:HALO, :] = jnp.where(i > 0, halo_ref[0], 0.0)
    ucat_ref[HALO:HALO + ts, :] = u_ref[0]
    cw = cw_ref[...]
    base = HALO - (CONV_WIDTH - 1)
    for rc in range(ts // rows_per_step):
        r0 = rc * rows_per_step
        acc = jnp.zeros((rows_per_step, CONV_CH), F32) + cb_ref[...]
        for k in range(CONV_WIDTH):
            acc = acc + cw[k:k + 1, :] * ucat_ref[r0 + base + k:r0 + base + k + rows_per_step, :]
        mu = jnp.mean(acc, axis=-1, keepdims=True)
        cen = acc - mu
        var = jnp.mean(cen * cen, axis=-1, keepdims=True)
        y = cen * lax.rsqrt(var + EPS) * lg_ref[...] + lb_ref[...]
        cv_ref[r0:r0 + rows_per_step, :] = (y * jax.nn.sigmoid(y)).astype(BF16)
    heads = jnp.concatenate([attn_ref[0], cv_ref[...]], axis=1)
    o_ref[0] = x_ref[0] + jnp.dot(heads, wo_ref[...], preferred_element_type=F32)


def _mix_out(x, attn, u, cw, cb, lg, lb, wo, *, ts):
    b, seq, d = x.shape
    blk = lambda width: pl.BlockSpec((1, ts, width), lambda bi, i: (bi, i, 0))
    halo = pl.BlockSpec((1, HALO, CONV_CH), lambda bi, i: (bi, jnp.maximum(i * (ts // HALO) - 1, 0), 0))
    return pl.pallas_call(
        functools.partial(_mix_out_kernel, rows_per_step=32),
        grid=(b, seq // ts),
        in_specs=[blk(d), blk(ATTN_WIDTH), blk(CONV_CH), halo,
                  _const_spec(cw.shape), _const_spec(cb.shape), _const_spec(lg.shape), _const_spec(lb.shape),
                  _const_spec(wo.shape)],
        out_specs=blk(d),
        out_shape=jax.ShapeDtypeStruct((b, seq, d), F32),
        scratch_shapes=[pltpu.VMEM((HALO + ts, CONV_CH), F32), pltpu.VMEM((ts, CONV_CH), BF16)],
        compiler_params=pltpu.CompilerParams(
            dimension_semantics=("parallel", "arbitrary"), vmem_limit_bytes=VMEM_LIMIT),
        name="conv_outproj",
    )(x, attn, u, u, cw, cb, lg, lb, wo)


def _prep_w_in(w_in):
    o_gl = ATTN_WIDTH + 6 * KV_WIDTH
    n_gl = N_HEADS * N_BRANCH
    head = w_in[..., :o_gl]
    gl = w_in[..., o_gl:o_gl + n_gl]
    rest = w_in[..., o_gl + n_gl:]
    gl = jnp.pad(gl, ((0, 0), (0, 0), (0, GATE_PAD - n_gl)))
    return jnp.concatenate([head, rest, gl], axis=-1).astype(BF16)


def _prep_cmp(pos, w1, w2):
    nl = pos.shape[0]
    half = CMP_STRIDE
    eye = jnp.eye(N_KV, dtype=w1.dtype)
    w1r = w1.reshape(nl, 2, half, HEAD_DIM, CMP_HIDDEN)
    w1x = w1r[:, :, :, None, :, None, :] * eye[:, None, :, None]
    w1x = w1x.reshape(nl, 2, half * N_KV * HEAD_DIM, N_KV * CMP_HIDDEN).astype(BF16)
    w2x = w2[:, None, :, None, :] * eye[:, None, :, None]
    w2x = w2x.reshape(nl, N_KV * CMP_HIDDEN, N_KV * HEAD_DIM).astype(BF16)
    pr = pos.reshape(nl, 2, half, 1, HEAD_DIM)
    px = jnp.broadcast_to(pr, (nl, 2, half, N_KV, HEAD_DIM)).reshape(nl, 2, 1, half * N_KV * HEAD_DIM)
    return px, w1x, w2x


def kernel(x, ffn1_norm, ffn1_w_gate, ffn1_w_up, ffn1_w_down, mix_norm, w_in, cmp_pos_k, cmp_k_w1, cmp_k_w2,
           cmp_pos_v, cmp_v_w1, cmp_v_w2, conv_w, conv_b, conv_norm_g, conv_norm_b, w_out, ffn2_norm,
           ffn2_w_gate, ffn2_w_up, ffn2_w_down, final_norm):
    b, seq, d = x.shape
    depth = ffn1_norm.shape[0]
    n = b * seq
    tm = 512
    ts = 256
    kchunk = 512
    assert d == D_MODEL and seq % kchunk == 0 and seq % tm == 0 and seq // SEL_BLOCK <= SEL_BLOCK

    bf = lambda w: w.astype(BF16)
    w1g, w1u, w1d = bf(ffn1_w_gate), bf(ffn1_w_up), bf(ffn1_w_down)
    w2g, w2u, w2d = bf(ffn2_w_gate), bf(ffn2_w_up), bf(ffn2_w_down)
    w_in_p = _prep_w_in(w_in)
    pk, w1k, w2k = _prep_cmp(cmp_pos_k, cmp_k_w1, cmp_k_w2)
    pv, w1v, w2v = _prep_cmp(cmp_pos_v, cmp_v_w1, cmp_v_w2)
    w_o = bf(w_out)
    row = lambda v: v.reshape(1, -1)
    fg = row(final_norm)
    kconst_slc, kconst_win, kconst_cmp, aux, ovt = _attn_tables(seq)

    def with_consts(k, consts):
        return jnp.concatenate([k, jnp.broadcast_to(consts, (b,) + consts.shape)], axis=-1)

    x2 = x.reshape(n, d)
    nchunk = seq // CMP_STRIDE
    r3 = lambda a: a.reshape(b, seq, a.shape[-1])
    tr = lambda a: jnp.swapaxes(a, 1, 2)
    tr1 = lambda a: jnp.pad(tr(a), ((0, 0), (0, ONES_ROWS), (0, 0)), constant_values=1)
    lpad = lambda a: jnp.pad(a, ((0, 0), (WINDOW, 0), (0, 0)))
    for l in range(depth):
        x2 = _ffn(x2, row(ffn1_norm[l]), w1g[l], w1u[l], w1d[l], fg, final=False, tm=tm)
        q, kc, vc, ks, vs, kw, vw, u, gates = _proj(x2, row(mix_norm[l]), w_in_p[l], tm=tm)
        kcc, vcc = _compress(kc.reshape(b, nchunk, CMP_STRIDE * KV_WIDTH),
                             vc.reshape(b, nchunk, CMP_STRIDE * KV_WIDTH),
                             pk[l], w1k[l], w2k[l], pv[l], w1v[l], w2v[l])
        attn = _attention(r3(q), with_consts(kcc, kconst_cmp), tr(vcc),
                          with_consts(r3(ks), kconst_slc), tr1(r3(vs)),
                          with_consts(lpad(r3(kw)), kconst_win), tr1(lpad(r3(vw))),
                          r3(gates), aux, ovt, kchunk=kchunk)
        x3 = _mix_out(x2.reshape(b, seq, d), attn, r3(u), conv_w[l], row(conv_b[l]), row(conv_norm_g[l]),
                      row(conv_norm_b[l]), w_o[l], ts=ts)
        x2 = _ffn(x3.reshape(n, d), row(ffn2_norm[l]), w2g[l], w2u[l], w2d[l], fg,
                  final=(l == depth - 1), tm=tm)
    return x2.reshape(b, seq, d)
```

```python
import functools

import jax
import jax.numpy as jnp
import numpy as np
from jax import lax
from jax.experimental import pallas as pl
from jax.experimental.pallas import tpu as pltpu

F32 = jnp.float32
BF16 = jnp.bfloat16

D_MODEL = 1024
N_HEADS = 8
HEAD_DIM = 64
N_KV = 2
GROUP = N_HEADS // N_KV
ATTN_WIDTH = N_HEADS * HEAD_DIM
KV_WIDTH = N_KV * HEAD_DIM
N_BRANCH = 3
CMP_BLOCK = 32
CMP_STRIDE = 16
CMP_HIDDEN = 256
SEL_BLOCK = 64
N_SEL = 16
WINDOW = 512
Q_BLOCK = 64
FORCE_BONUS = 1e3
CONV_CH = D_MODEL - ATTN_WIDTH
CONV_WIDTH = 31
D_FF = 2816
EPS = 1e-6
NEG = -1e30

V7X_VMEM_BYTES = 64 * 1024 * 1024
VMEM_LIMIT = 56 * 1024 * 1024
LANES = 128
SUBLANES = 8

GATE_PAD = LANES
PROJ_WIDTH = ATTN_WIDTH + 6 * KV_WIDTH + 2 * CONV_CH + GATE_PAD
HALO = 32
ROWS = GROUP * Q_BLOCK
COLS = N_KV * ROWS
WIN_SPAN = WINDOW + LANES
STRIP = 64
ONES_ROWS = 16
VT_ROWS = KV_WIDTH + ONES_ROWS

AUX_BLK = SEL_BLOCK
AUX_OFF = SEL_BLOCK + 1
AUX_ONE_C = SEL_BLOCK + 2
AUX_ONE_Q = SEL_BLOCK + 3
AUX_CMP_N = SEL_BLOCK + 4
AUX_CMP_ONE = SEL_BLOCK + 5
AUX_PAD = SEL_BLOCK + 6


def _const_spec(shape):
    n = len(shape)
    return pl.BlockSpec(shape, lambda *_: (0,) * n, pipeline_mode=pl.Buffered(1))


def _rms(x, g):
    ms = jnp.mean(x * x, axis=-1, keepdims=True)
    return x * lax.rsqrt(ms + EPS) * g


def _ffn_kernel(x_ref, g_ref, wg_ref, wu_ref, wd_ref, fg_ref, o_ref, *, final):
    x = x_ref[...]
    h = _rms(x, g_ref[...]).astype(BF16)
    a = jnp.dot(h, wg_ref[...], preferred_element_type=F32)
    u = jnp.dot(h, wu_ref[...], preferred_element_type=F32)
    act = (a * jax.nn.sigmoid(a) * u).astype(BF16)
    y = jnp.dot(act, wd_ref[...], preferred_element_type=F32)
    o = x + 0.5 * y
    if final:
        o = _rms(o, fg_ref[...])
    o_ref[...] = o


def _ffn(x2, g, wg, wu, wd, fg, *, final, tm):
    n, d = x2.shape
    f = wg.shape[1]
    return pl.pallas_call(
        functools.partial(_ffn_kernel, final=final),
        grid=(n // tm,),
        in_specs=[
            pl.BlockSpec((tm, d), lambda i: (i, 0)),
            _const_spec((1, d)),
            _const_spec((d, f)),
            _const_spec((d, f)),
            _const_spec((f, d)),
            _const_spec((1, d)),
        ],
        out_specs=pl.BlockSpec((tm, d), lambda i: (i, 0)),
        out_shape=jax.ShapeDtypeStruct((n, d), F32),
        compiler_params=pltpu.CompilerParams(
            dimension_semantics=("parallel",), vmem_limit_bytes=VMEM_LIMIT),
        name="ffn",
    )(x2, g, wg, wu, wd, fg)


def _proj_kernel(x_ref, g_ref, w_ref, kcs_ref, kcw_ref, q_ref, kc_ref, vc_ref, ks_ref, vs_ref, kw_ref, vw_ref,
                 u_ref, gate_ref):
    h = _rms(x_ref[...], g_ref[...]).astype(BF16)
    z = jnp.dot(h, w_ref[...], preferred_element_type=F32)
    o = 0
    q_ref[...] = (z[:, o:o + ATTN_WIDTH] * (HEAD_DIM ** -0.5)).astype(BF16)
    o += ATTN_WIDTH
    kc_ref[...] = z[:, o:o + KV_WIDTH]
    o += KV_WIDTH
    vc_ref[...] = z[:, o:o + KV_WIDTH]
    o += KV_WIDTH
    ks_ref[:, :KV_WIDTH] = z[:, o:o + KV_WIDTH].astype(BF16)
    ks_ref[:, KV_WIDTH:] = kcs_ref[...]
    o += KV_WIDTH
    vs_ref[...] = z[:, o:o + KV_WIDTH].astype(BF16)
    o += KV_WIDTH
    kw_ref[0, :, :KV_WIDTH] = z[:, o:o + KV_WIDTH].astype(BF16)
    kw_ref[0, :, KV_WIDTH:] = kcw_ref[...]
    o += KV_WIDTH
    vw_ref[...] = z[:, o:o + KV_WIDTH].astype(BF16)
    o += KV_WIDTH
    ga = z[:, o:o + CONV_CH]
    o += CONV_CH
    gb = z[:, o:o + CONV_CH]
    o += CONV_CH
    u_ref[...] = ga * jax.nn.sigmoid(gb)
    gate_ref[...] = jax.nn.sigmoid(z[:, o:o + GATE_PAD])


def _proj(x2, g, w, kconst_slc, kconst_win, *, batch, tm):
    n, d = x2.shape
    seq = n // batch
    tiles = seq // tm
    lead = WINDOW // tm
    row = lambda width: pl.BlockSpec((tm, width), lambda i: (i, 0))
    kconst = pl.BlockSpec((tm, LANES), lambda i: (i % tiles, 0))
    shp = lambda width, dt: jax.ShapeDtypeStruct((n, width), dt)
    kw_spec = pl.BlockSpec((1, tm, KV_WIDTH + LANES), lambda i: (i // tiles, i % tiles + lead, 0))
    kw_shape = jax.ShapeDtypeStruct((batch, WINDOW + seq, KV_WIDTH + LANES), BF16)
    return pl.pallas_call(
        _proj_kernel,
        grid=(n // tm,),
        in_specs=[row(d), _const_spec((1, d)), _const_spec((d, PROJ_WIDTH)), kconst, kconst],
        out_specs=[row(ATTN_WIDTH), row(KV_WIDTH), row(KV_WIDTH), row(KV_WIDTH + LANES), row(KV_WIDTH),
                   kw_spec, row(KV_WIDTH), row(CONV_CH), row(GATE_PAD)],
        out_shape=[shp(ATTN_WIDTH, BF16), shp(KV_WIDTH, F32), shp(KV_WIDTH, F32),
                   shp(KV_WIDTH + LANES, BF16), shp(KV_WIDTH, BF16), kw_shape, shp(KV_WIDTH, BF16),
                   shp(CONV_CH, F32), shp(GATE_PAD, F32)],
        compiler_params=pltpu.CompilerParams(
            dimension_semantics=("parallel",), vmem_limit_bytes=VMEM_LIMIT),
        name="proj",
    )(x2, g, w, kconst_slc, kconst_win[WINDOW:])


def _cmp_kernel(kf_ref, vf_ref, pk_ref, w1k_ref, w2k_ref, pv_ref, w1v_ref, w2v_ref, ko_ref, vo_ref):
    nchunk = kf_ref.shape[1]
    row = lax.broadcasted_iota(jnp.int32, (nchunk, 1), 0)
    for f_ref, p_ref, w1_ref, w2_ref, o_ref in ((kf_ref, pk_ref, w1k_ref, w2k_ref, ko_ref),
                                               (vf_ref, pv_ref, w1v_ref, w2v_ref, vo_ref)):
        c = f_ref[0]
        ha = jnp.dot((c + p_ref[0]).astype(BF16), w1_ref[0], preferred_element_type=F32)
        hb = jnp.dot((c + p_ref[1]).astype(BF16), w1_ref[1], preferred_element_type=F32)
        hid = ha + pltpu.roll(hb, nchunk - 1, 0)
        hid = jnp.where(row < nchunk - 1, hid, 0.0)
        act = (hid * jax.nn.sigmoid(hid)).astype(BF16)
        o_ref[0] = jnp.dot(act, w2_ref[...], preferred_element_type=F32).astype(BF16)


def _compress(kf, vf, pk, w1k, w2k, pv, w1v, w2v):
    b, nchunk, width = kf.shape
    blk = pl.BlockSpec((1, nchunk, width), lambda i: (i, 0, 0))
    oblk = pl.BlockSpec((1, nchunk, KV_WIDTH), lambda i: (i, 0, 0))
    wspecs = [_const_spec(pk.shape), _const_spec(w1k.shape), _const_spec(w2k.shape)]
    return pl.pallas_call(
        _cmp_kernel,
        grid=(b,),
        in_specs=[blk, blk] + wspecs + wspecs,
        out_specs=[oblk, oblk],
        out_shape=[jax.ShapeDtypeStruct((b, nchunk, KV_WIDTH), BF16)] * 2,
        compiler_params=pltpu.CompilerParams(
            dimension_semantics=("parallel",), vmem_limit_bytes=VMEM_LIMIT),
        name="compress",
    )(kf, vf, pk, w1k, w2k, pv, w1v, w2v)


def _attn_kernel(q_ref, kc_ref, vct_ref, ks_ref, vst_ref, kw_ref, vwt_ref, gate_ref, aux_ref, ovt_ref,
                 o_ref, s0_scr, s1_scr, e0_scr, e1_scr, acc_scr, *, kchunk):
    ncmp = kc_ref.shape[1]
    nsel = ovt_ref.shape[0] - ONES_ROWS
    c = pl.program_id(1)
    q0 = c * Q_BLOCK
    c_f = c.astype(F32)
    nt = (((1,), (1,)), ((), ()))
    qf = q_ref[0].astype(F32)
    t_row = q0 + (lax.broadcasted_iota(jnp.int32, (1, COLS), 1) & (Q_BLOCK - 1))
    zeros_half = jnp.zeros((ROWS, HEAD_DIM), F32)

    qpad, aux0 = [], []
    for g in range(N_KV):
        qg = jnp.concatenate(
            [qf[:, (g * GROUP + r) * HEAD_DIM:(g * GROUP + r + 1) * HEAD_DIM] for r in range(GROUP)],
            axis=0)
        qpad.append(jnp.concatenate([qg, zeros_half] if g == 0 else [zeros_half, qg], axis=1))
        aux0.append(aux_ref[g, 0] + c_f * aux_ref[g, 1])
    qpad = jnp.concatenate(qpad, axis=0)
    aux0 = jnp.concatenate(aux0, axis=0)
    qp0 = jnp.concatenate([qpad, aux0], axis=1).astype(BF16)

    cmp_end = lax.broadcasted_iota(jnp.int32, (ncmp, 1), 0) * CMP_STRIDE + (CMP_BLOCK - 1)
    cmp_valid = cmp_end <= t_row
    has_cmp = t_row >= CMP_BLOCK - 1
    ovt = ovt_ref[...]
    s = lax.dot_general(kc_ref[0], qp0, nt, preferred_element_type=F32)
    s = jnp.where(cmp_valid, s, NEG)
    e = jnp.exp(s - jnp.max(s, axis=0, keepdims=True))
    e_hi = e.astype(BF16)
    e_lo = (e - e_hi.astype(F32)).astype(BF16)
    red = (jnp.dot(ovt, e_hi, preferred_element_type=F32)
           + jnp.dot(ovt, e_lo, preferred_element_type=F32))
    rinv = jnp.where(has_cmp, 1.0 / red[nsel:nsel + 1], 0.0)
    o_cmp = jnp.dot(vct_ref[0], e_hi, preferred_element_type=F32) * rinv
    imp_t = red[:nsel] * rinv
    imp = []
    for g in range(N_KV):
        half = imp_t[:, g * ROWS:g * ROWS + LANES] + imp_t[:, g * ROWS + LANES:(g + 1) * ROWS]
        imp.append(half + pltpu.roll(half, Q_BLOCK, 1))

    lane = lax.broadcasted_iota(jnp.int32, (1, LANES), 1)
    blk = lax.broadcasted_iota(jnp.int32, (nsel, 1), 0)
    forced = (blk == 0) | (blk == c) | (blk == c - 1)
    blk_valid = blk <= c
    score = jnp.where(lane < Q_BLOCK, imp[0], imp[1])
    score = jnp.where(blk_valid, score + jnp.where(forced, FORCE_BONUS, 0.0), -jnp.inf)
    j_loc = lax.broadcasted_iota(jnp.int32, (SUBLANES, 1), 0)
    nv = nsel // SUBLANES
    tiles = [score[v * SUBLANES:(v + 1) * SUBLANES, :] for v in range(nv)]
    rank = [jnp.zeros((SUBLANES, LANES), F32) for _ in range(nv)]
    for i in range(nsel):
        row = score[i:i + 1, :]
        for v in range(nv):
            if v * SUBLANES + SUBLANES - 1 < i:
                before = jnp.where(row > tiles[v], 1.0, 0.0)
            elif v * SUBLANES > i:
                before = jnp.where(row >= tiles[v], 1.0, 0.0)
            else:
                later = jnp.where(j_loc > i - v * SUBLANES, 1.0, 0.0)
                before = jnp.where(row > tiles[v], 1.0, jnp.where(row == tiles[v], later, 0.0))
            rank[v] = rank[v] + before
    rank = jnp.concatenate(rank, axis=0)
    sel_bias = jnp.where(blk_valid, jnp.where(rank < float(min(N_SEL, nsel)), 0.0, NEG), NEG)
    sel_bias = jnp.concatenate([sel_bias, jnp.zeros((LANES - nsel, LANES), F32)], axis=0).T

    gate_t = jnp.concatenate([gate_ref[0], jnp.zeros((LANES - Q_BLOCK, GATE_PAD), F32)], axis=0).T

    w_al = pl.multiple_of((q0 // LANES) * LANES, LANES)
    kr_top = w_al + lax.broadcasted_iota(jnp.int32, (LANES, 1), 0)
    kr_bot = kr_top + WINDOW
    s = lax.dot_general(kw_ref[0, pl.ds(w_al, WIN_SPAN), :], qp0, nt, preferred_element_type=F32)
    s = jnp.concatenate([jnp.where(kr_top > t_row, s[:LANES], NEG),
                         s[LANES:WINDOW],
                         jnp.where(kr_bot <= t_row + WINDOW, s[WINDOW:], NEG)], axis=0)
    e = jnp.exp(s - jnp.max(s, axis=0, keepdims=True))
    res = jnp.dot(vwt_ref[0, :, pl.ds(w_al, WIN_SPAN)], e.astype(BF16), preferred_element_type=F32)
    o_win = res[:KV_WIDTH] * (1.0 / res[KV_WIDTH:KV_WIDTH + 1])

    sel_rows = jnp.concatenate(
        [sel_bias[g * Q_BLOCK:(g + 1) * Q_BLOCK, :] for g in range(N_KV) for _ in range(GROUP)], axis=0)
    qp = jnp.concatenate([qpad, aux0 + sel_rows], axis=1).astype(BF16)
    s_bufs = (s0_scr, s1_scr)
    e_bufs = (e0_scr, e1_scr)
    n_strip = kchunk // STRIP
    n_full = q0 // kchunk

    def produce(k0, s_buf):
        s = lax.dot_general(ks_ref[0, pl.ds(k0, kchunk), :], qp, nt, preferred_element_type=F32)
        s_buf[...] = s
        return jnp.max(s, axis=0, keepdims=True)

    def soften(s_buf, e_buf, smax, m):
        m_new = jnp.maximum(m, smax)
        for j in range(n_strip):
            rows = slice(j * STRIP, (j + 1) * STRIP)
            e_buf[rows, :] = jnp.exp(s_buf[rows, :] - m_new).astype(BF16)
        return m_new, jnp.exp(m - m_new)

    def add_values(k0, e_buf, alpha):
        vb = vst_ref[0, :, pl.ds(k0, kchunk)]
        acc_scr[...] = alpha * acc_scr[...] + jnp.dot(vb, e_buf[...], preferred_element_type=F32)

    def step(j, carry, cur, nxt):
        smax, m, alpha_prev = carry
        k0 = pl.multiple_of(j * kchunk, kchunk)
        add_values(pl.multiple_of(jnp.maximum(k0 - kchunk, 0), kchunk), e_bufs[nxt], alpha_prev)
        m, alpha = soften(s_bufs[cur], e_bufs[cur], smax, m)
        return produce(pl.multiple_of(k0 + kchunk, kchunk), s_bufs[nxt]), m, alpha

    def start(cur, nxt):
        e_bufs[nxt][...] = jnp.zeros(e_bufs[nxt].shape, BF16)
        return produce(0, s_bufs[cur])

    def slc_body(j, carry):
        return lax.cond(((n_full - j) & 1) == 0,
                        lambda cr: step(j, cr, 0, 1), lambda cr: step(j, cr, 1, 0), carry)

    acc_scr[...] = jnp.zeros(acc_scr.shape, F32)
    smax0 = lax.cond((n_full & 1) == 0, lambda: start(0, 1), lambda: start(1, 0))
    carry0 = (smax0, jnp.full((1, COLS), NEG, F32), jnp.ones((1, COLS), F32))
    _, m_run, alpha_prev = lax.fori_loop(0, n_full, slc_body, carry0)
    k_last = pl.multiple_of(n_full * kchunk, kchunk)
    add_values(pl.multiple_of(jnp.maximum(k_last - kchunk, 0), kchunk), e1_scr, alpha_prev)
    diag = pl.multiple_of(q0 - k_last, Q_BLOCK)
    causal = (q0 + lax.broadcasted_iota(jnp.int32, (Q_BLOCK, 1), 0)) <= t_row
    s0_scr[pl.ds(diag, Q_BLOCK), :] = jnp.where(causal, s0_scr[pl.ds(diag, Q_BLOCK), :], NEG)
    mx = s0_scr[0:STRIP, :]
    for j in range(1, n_strip):
        mx = jnp.maximum(mx, s0_scr[j * STRIP:(j + 1) * STRIP, :])
    _, alpha_last = soften(s0_scr, e0_scr, jnp.max(mx, axis=0, keepdims=True), m_run)
    add_values(k_last, e0_scr, alpha_last)
    o_slc = acc_scr[:KV_WIDTH, :] * (1.0 / acc_scr[KV_WIDTH:KV_WIDTH + 1, :])

    def gate_row(br):
        return jnp.concatenate([gate_t[h * N_BRANCH + br:h * N_BRANCH + br + 1, :Q_BLOCK]
                                for h in range(N_HEADS)], axis=1)

    og = (gate_row(0) * o_cmp + gate_row(1) * o_slc + gate_row(2) * o_win).T
    o_ref[0] = jnp.concatenate(
        [og[h * Q_BLOCK:(h + 1) * Q_BLOCK, (h // GROUP) * HEAD_DIM:(h // GROUP + 1) * HEAD_DIM]
         for h in range(N_HEADS)], axis=1).astype(BF16)


def _attention(q, kc, vct, ks, vst, kw, vwt, gates, aux, ovt, *, kchunk):
    b, seq, _ = q.shape
    qspec = lambda width: pl.BlockSpec((1, Q_BLOCK, width), lambda i, j: (i, j, 0))
    full = lambda a: pl.BlockSpec((1,) + a.shape[1:], lambda i, j: (i, 0, 0))
    return pl.pallas_call(
        functools.partial(_attn_kernel, kchunk=kchunk),
        grid=(b, seq // Q_BLOCK),
        in_specs=[qspec(ATTN_WIDTH), full(kc), full(vct), full(ks), full(vst), full(kw), full(vwt),
                  qspec(GATE_PAD), _const_spec(aux.shape), _const_spec(ovt.shape)],
        out_specs=qspec(ATTN_WIDTH),
        out_shape=jax.ShapeDtypeStruct((b, seq, ATTN_WIDTH), BF16),
        scratch_shapes=[pltpu.VMEM((kchunk, COLS), F32), pltpu.VMEM((kchunk, COLS), F32),
                        pltpu.VMEM((kchunk, COLS), BF16), pltpu.VMEM((kchunk, COLS), BF16),
                        pltpu.VMEM((VT_ROWS, COLS), F32)],
        compiler_params=pltpu.CompilerParams(
            dimension_semantics=("parallel", "arbitrary"), vmem_limit_bytes=VMEM_LIMIT),
        name="nsa_attention",
    )(q, kc, vct, ks, vst, kw, vwt, gates, aux, ovt)


def _attn_tables(seq):
    nsel = seq // SEL_BLOCK
    ncmp = seq // CMP_STRIDE
    lane = np.arange(LANES)[None, :]

    def key_consts(pos, is_pad):
        blk = (pos // SEL_BLOCK)[:, None]
        off = (pos % SEL_BLOCK)[:, None]
        live = (~is_pad)[:, None]
        t = np.where((lane == blk) & (lane < SEL_BLOCK), 1.0, 0.0)
        t = t + np.where(lane == AUX_BLK, blk, 0) + np.where(lane == AUX_OFF, off, 0)
        t = t + np.where((lane == AUX_ONE_C) | (lane == AUX_ONE_Q), 1.0, 0.0)
        return np.where(live, t, np.where(lane == AUX_PAD, 1.0, 0.0)).astype(np.float32)

    pos = np.arange(seq)
    k_slc = key_consts(pos, pos < 0)
    posw = np.arange(seq + WINDOW) - WINDOW
    k_win = key_consts(np.maximum(posw, 0), posw < 0)
    k_win = np.where(lane < SEL_BLOCK, 0.0, k_win).astype(np.float32)
    n = np.arange(ncmp)[:, None]
    k_cmp = (np.where((lane == AUX_ONE_C) | (lane == AUX_ONE_Q) | (lane == AUX_CMP_ONE), 1.0, 0.0)
             + np.where(lane == AUX_CMP_N, n, 0)).astype(np.float32)

    row = np.arange(ROWS)
    qi = (row % Q_BLOCK).astype(np.float32)[:, None]
    aux = []
    for g in range(N_KV):
        slope = np.exp2(-(g * GROUP + row // Q_BLOCK + 1).astype(np.float32))[:, None]
        t0 = (np.where(lane == AUX_BLK, slope * SEL_BLOCK, 0.0) + np.where(lane == AUX_OFF, slope, 0.0)
              + np.where(lane == AUX_ONE_Q, -slope * qi, 0.0)
              + np.where(lane == AUX_CMP_N, slope * CMP_STRIDE, 0.0)
              + np.where(lane == AUX_CMP_ONE, slope * (CMP_BLOCK - 1), 0.0)
              + np.where(lane == AUX_PAD, NEG, 0.0))
        t1 = np.where(lane == AUX_ONE_C, -slope * Q_BLOCK, 0.0)
        aux.append(np.stack([t0, t1]))
    aux = np.stack(aux).astype(np.float32)

    cmp_start = np.arange(ncmp) * CMP_STRIDE
    sel_start = np.arange(nsel) * SEL_BLOCK
    ovt = np.clip(np.minimum(cmp_start[None] + CMP_BLOCK, sel_start[:, None] + SEL_BLOCK)
                  - np.maximum(cmp_start[None], sel_start[:, None]), 0, None).astype(np.float32) / CMP_BLOCK
    ovt = np.concatenate([ovt, np.ones((ONES_ROWS, ncmp), np.float32)], axis=0)
    bf = lambda a: jnp.asarray(a, BF16)
    return bf(k_slc), bf(k_win), bf(k_cmp), jnp.asarray(aux), bf(ovt)


def _mix_out_kernel(x_ref, attn_ref, u_ref, halo_ref, cw_ref, cb_ref, lg_ref, lb_ref, wo_ref, o_ref,
                    ush_ref, cv_ref, *, rows_per_step):
    ts = u_ref.shape[1]
    span = HALO + ts
    i = pl.program_id(1)
    ush_ref[0, 0:HALO, :] = jnp.where(i > 0, halo_ref[0], 0.0)
    ush_ref[0, HALO:span, :] = u_ref[0]
    for b in range(1, SUBLANES):
        ush_ref[b, 0:span - SUBLANES, :] = ush_ref[0, b:span - SUBLANES + b, :]
    cw = cw_ref[...]
    base = HALO - (CONV_WIDTH - 1)
    for rc in range(ts // rows_per_step):
        r0 = rc * rows_per_step
        acc = jnp.zeros((rows_per_step, CONV_CH), F32) + cb_ref[...]
        for k in range(CONV_WIDTH):
            shift, start = (base + k) % SUBLANES, (base + k) // SUBLANES * SUBLANES
            acc = acc + cw[k:k + 1, :] * ush_ref[shift, r0 + start:r0 + start + rows_per_step, :]
        mu = jnp.mean(acc, axis=-1, keepdims=True)
        cen = acc - mu
        var = jnp.mean(cen * cen, axis=-1, keepdims=True)
        y = cen * lax.rsqrt(var + EPS) * lg_ref[...] + lb_ref[...]
        cv_ref[r0:r0 + rows_per_step, :] = (y * jax.nn.sigmoid(y)).astype(BF16)
    heads = jnp.concatenate([attn_ref[0], cv_ref[...]], axis=1)
    o_ref[0] = x_ref[0] + jnp.dot(heads, wo_ref[...], preferred_element_type=F32)


def _mix_out(x, attn, u, cw, cb, lg, lb, wo, *, ts):
    b, seq, d = x.shape
    blk = lambda width: pl.BlockSpec((1, ts, width), lambda bi, i: (bi, i, 0))
    halo = pl.BlockSpec((1, HALO, CONV_CH), lambda bi, i: (bi, jnp.maximum(i * (ts // HALO) - 1, 0), 0))
    return pl.pallas_call(
        functools.partial(_mix_out_kernel, rows_per_step=32),
        grid=(b, seq // ts),
        in_specs=[blk(d), blk(ATTN_WIDTH), blk(CONV_CH), halo,
                  _const_spec(cw.shape), _const_spec(cb.shape), _const_spec(lg.shape), _const_spec(lb.shape),
                  _const_spec(wo.shape)],
        out_specs=blk(d),
        out_shape=jax.ShapeDtypeStruct((b, seq, d), F32),
        scratch_shapes=[pltpu.VMEM((SUBLANES, HALO + ts, CONV_CH), F32), pltpu.VMEM((ts, CONV_CH), BF16)],
        compiler_params=pltpu.CompilerParams(
            dimension_semantics=("parallel", "arbitrary"), vmem_limit_bytes=VMEM_LIMIT),
        name="conv_outproj",
    )(x, attn, u, u, cw, cb, lg, lb, wo)


def _prep_w_in(w_in):
    o_gl = ATTN_WIDTH + 6 * KV_WIDTH
    n_gl = N_HEADS * N_BRANCH
    head = w_in[..., :o_gl]
    gl = w_in[..., o_gl:o_gl + n_gl]
    rest = w_in[..., o_gl + n_gl:]
    gl = jnp.pad(gl, ((0, 0), (0, 0), (0, GATE_PAD - n_gl)))
    return jnp.concatenate([head, rest, gl], axis=-1).astype(BF16)


def _prep_cmp(pos, w1, w2):
    nl = pos.shape[0]
    half = CMP_STRIDE
    eye = jnp.eye(N_KV, dtype=w1.dtype)
    w1r = w1.reshape(nl, 2, half, HEAD_DIM, CMP_HIDDEN)
    w1x = w1r[:, :, :, None, :, None, :] * eye[:, None, :, None]
    w1x = w1x.reshape(nl, 2, half * N_KV * HEAD_DIM, N_KV * CMP_HIDDEN).astype(BF16)
    w2x = w2[:, None, :, None, :] * eye[:, None, :, None]
    w2x = w2x.reshape(nl, N_KV * CMP_HIDDEN, N_KV * HEAD_DIM).astype(BF16)
    pr = pos.reshape(nl, 2, half, 1, HEAD_DIM)
    px = jnp.broadcast_to(pr, (nl, 2, half, N_KV, HEAD_DIM)).reshape(nl, 2, 1, half * N_KV * HEAD_DIM)
    return px, w1x, w2x


def kernel(x, ffn1_norm, ffn1_w_gate, ffn1_w_up, ffn1_w_down, mix_norm, w_in, cmp_pos_k, cmp_k_w1, cmp_k_w2,
           cmp_pos_v, cmp_v_w1, cmp_v_w2, conv_w, conv_b, conv_norm_g, conv_norm_b, w_out, ffn2_norm,
           ffn2_w_gate, ffn2_w_up, ffn2_w_down, final_norm):
    b, seq, d = x.shape
    depth = ffn1_norm.shape[0]
    n = b * seq
    tm = 512
    ts = 256
    kchunk = 512
    assert d == D_MODEL and seq % kchunk == 0 and seq % tm == 0 and seq // SEL_BLOCK <= SEL_BLOCK
    assert WINDOW % tm == 0

    bf = lambda w: w.astype(BF16)
    w1g, w1u, w1d = bf(ffn1_w_gate), bf(ffn1_w_up), bf(ffn1_w_down)
    w2g, w2u, w2d = bf(ffn2_w_gate), bf(ffn2_w_up), bf(ffn2_w_down)
    w_in_p = _prep_w_in(w_in)
    pk, w1k, w2k = _prep_cmp(cmp_pos_k, cmp_k_w1, cmp_k_w2)
    pv, w1v, w2v = _prep_cmp(cmp_pos_v, cmp_v_w1, cmp_v_w2)
    w_o = bf(w_out)
    row = lambda v: v.reshape(1, -1)
    fg = row(final_norm)
    kconst_slc, kconst_win, kconst_cmp, aux, ovt = _attn_tables(seq)

    def with_consts(k, consts):
        return jnp.concatenate([k, jnp.broadcast_to(consts, (b,) + consts.shape)], axis=-1)

    x2 = x.reshape(n, d)
    nchunk = seq // CMP_STRIDE
    r3 = lambda a: a.reshape(b, seq, a.shape[-1])
    tr = lambda a: jnp.swapaxes(a, 1, 2)
    tr1 = lambda a: jnp.pad(tr(a), ((0, 0), (0, ONES_ROWS), (0, 0)), constant_values=1)
    lpad = lambda a: jnp.pad(a, ((0, 0), (WINDOW, 0), (0, 0)))
    kw_pad = jnp.concatenate([jnp.zeros((WINDOW, KV_WIDTH), BF16), kconst_win[:WINDOW]], axis=1)
    kw_pad = jnp.broadcast_to(kw_pad, (b,) + kw_pad.shape)
    for l in range(depth):
        x2 = _ffn(x2, row(ffn1_norm[l]), w1g[l], w1u[l], w1d[l], fg, final=False, tm=tm)
        q, kc, vc, ks, vs, kw, vw, u, gates = _proj(x2, row(mix_norm[l]), w_in_p[l], kconst_slc, kconst_win,
                                                    batch=b, tm=tm)
        kw = lax.dynamic_update_slice(kw, kw_pad, (0, 0, 0))
        kcc, vcc = _compress(kc.reshape(b, nchunk, CMP_STRIDE * KV_WIDTH),
                             vc.reshape(b, nchunk, CMP_STRIDE * KV_WIDTH),
                             pk[l], w1k[l], w2k[l], pv[l], w1v[l], w2v[l])
        attn = _attention(r3(q), with_consts(kcc, kconst_cmp), tr(vcc),
                          r3(ks), tr1(r3(vs)), kw, tr1(lpad(r3(vw))),
                          r3(gates), aux, ovt, kchunk=kchunk)
        x3 = _mix_out(x2.reshape(b, seq, d), attn, r3(u), conv_w[l], row(conv_b[l]), row(conv_norm_g[l]),
                      row(conv_norm_b[l]), w_o[l], ts=ts)
        x2 = _ffn(x3.reshape(n, d), row(ffn2_norm[l]), w2g[l], w2u[l], w2d[l], fg,
                  final=(l == depth - 1), tm=tm)
    return x2.reshape(b, seq, d)
```

```python
import functools

import jax
import jax.numpy as jnp
import numpy as np
from jax import lax
from jax.experimental import pallas as pl
from jax.experimental.pallas import tpu as pltpu

F32 = jnp.float32
BF16 = jnp.bfloat16

D_MODEL = 1024
N_HEADS = 8
HEAD_DIM = 64
N_KV = 2
GROUP = N_HEADS // N_KV
ATTN_WIDTH = N_HEADS * HEAD_DIM
KV_WIDTH = N_KV * HEAD_DIM
N_BRANCH = 3
CMP_BLOCK = 32
CMP_STRIDE = 16
CMP_HIDDEN = 256
SEL_BLOCK = 64
N_SEL = 16
WINDOW = 512
Q_BLOCK = 64
FORCE_BONUS = 1e3
CONV_CH = D_MODEL - ATTN_WIDTH
CONV_WIDTH = 31
D_FF = 2816
EPS = 1e-6
NEG = -1e30

V7X_VMEM_BYTES = 64 * 1024 * 1024
VMEM_LIMIT = 56 * 1024 * 1024
LANES = 128
SUBLANES = 8

GATE_PAD = LANES
PROJ_WIDTH = ATTN_WIDTH + 6 * KV_WIDTH + 2 * CONV_CH + GATE_PAD
HALO = 32
ROWS = GROUP * Q_BLOCK
COLS = N_KV * ROWS
Q_STEP = 4 * Q_BLOCK
WIN_SPAN = WINDOW + Q_STEP
STRIP = 16
ONES_ROWS = 16
VT_ROWS = KV_WIDTH + ONES_ROWS

AUX_BLK = SEL_BLOCK
AUX_OFF = SEL_BLOCK + 1
AUX_ONE_C = SEL_BLOCK + 2
AUX_ONE_Q = SEL_BLOCK + 3
AUX_CMP_N = SEL_BLOCK + 4
AUX_CMP_ONE = SEL_BLOCK + 5
AUX_PAD = SEL_BLOCK + 6


def _const_spec(shape):
    n = len(shape)
    return pl.BlockSpec(shape, lambda *_: (0,) * n, pipeline_mode=pl.Buffered(1))


def _rms(x, g):
    ms = jnp.mean(x * x, axis=-1, keepdims=True)
    return x * lax.rsqrt(ms + EPS) * g


def _ffn_kernel(x_ref, g_ref, wg_ref, wu_ref, wd_ref, fg_ref, o_ref, *, final):
    x = x_ref[...]
    h = _rms(x, g_ref[...]).astype(BF16)
    a = jnp.dot(h, wg_ref[...], preferred_element_type=F32)
    u = jnp.dot(h, wu_ref[...], preferred_element_type=F32)
    act = (a * jax.nn.sigmoid(a) * u).astype(BF16)
    y = jnp.dot(act, wd_ref[...], preferred_element_type=F32)
    o = x + 0.5 * y
    if final:
        o = _rms(o, fg_ref[...])
    o_ref[...] = o


def _ffn(x2, g, wg, wu, wd, fg, *, final, tm):
    n, d = x2.shape
    f = wg.shape[1]
    return pl.pallas_call(
        functools.partial(_ffn_kernel, final=final),
        grid=(n // tm,),
        in_specs=[
            pl.BlockSpec((tm, d), lambda i: (i, 0)),
            _const_spec((1, d)),
            _const_spec((d, f)),
            _const_spec((d, f)),
            _const_spec((f, d)),
            _const_spec((1, d)),
        ],
        out_specs=pl.BlockSpec((tm, d), lambda i: (i, 0)),
        out_shape=jax.ShapeDtypeStruct((n, d), F32),
        compiler_params=pltpu.CompilerParams(
            dimension_semantics=("parallel",), vmem_limit_bytes=VMEM_LIMIT),
        name="ffn",
    )(x2, g, wg, wu, wd, fg)


def _proj_kernel(x_ref, g_ref, w_ref, kcs_ref, kcw_ref, q_ref, kc_ref, vc_ref, ks_ref, vs_ref, kw_ref, vw_ref,
                 u_ref, gate_ref):
    h = _rms(x_ref[...], g_ref[...]).astype(BF16)
    z = jnp.dot(h, w_ref[...], preferred_element_type=F32)
    o = 0
    q_ref[...] = (z[:, o:o + ATTN_WIDTH] * (HEAD_DIM ** -0.5)).astype(BF16)
    o += ATTN_WIDTH
    kc_ref[...] = z[:, o:o + KV_WIDTH]
    o += KV_WIDTH
    vc_ref[...] = z[:, o:o + KV_WIDTH]
    o += KV_WIDTH
    ks_ref[:, :KV_WIDTH] = z[:, o:o + KV_WIDTH].astype(BF16)
    ks_ref[:, KV_WIDTH:] = kcs_ref[...]
    o += KV_WIDTH
    vs_ref[...] = z[:, o:o + KV_WIDTH].astype(BF16)
    o += KV_WIDTH
    kw_ref[0, :, :KV_WIDTH] = z[:, o:o + KV_WIDTH].astype(BF16)
    kw_ref[0, :, KV_WIDTH:] = kcw_ref[...]
    o += KV_WIDTH
    vw_ref[...] = z[:, o:o + KV_WIDTH].astype(BF16)
    o += KV_WIDTH
    ga = z[:, o:o + CONV_CH]
    o += CONV_CH
    gb = z[:, o:o + CONV_CH]
    o += CONV_CH
    u_ref[...] = ga * jax.nn.sigmoid(gb)
    gate_ref[...] = jax.nn.sigmoid(z[:, o:o + GATE_PAD])


def _proj(x2, g, w, kconst_slc, kconst_win, *, batch, tm):
    n, d = x2.shape
    seq = n // batch
    tiles = seq // tm
    lead = WINDOW // tm
    row = lambda width: pl.BlockSpec((tm, width), lambda i: (i, 0))
    kconst = pl.BlockSpec((tm, LANES), lambda i: (i % tiles, 0))
    shp = lambda width, dt: jax.ShapeDtypeStruct((n, width), dt)
    kw_spec = pl.BlockSpec((1, tm, KV_WIDTH + LANES), lambda i: (i // tiles, i % tiles + lead, 0))
    kw_shape = jax.ShapeDtypeStruct((batch, WINDOW + seq, KV_WIDTH + LANES), BF16)
    return pl.pallas_call(
        _proj_kernel,
        grid=(n // tm,),
        in_specs=[row(d), _const_spec((1, d)), _const_spec((d, PROJ_WIDTH)), kconst, kconst],
        out_specs=[row(ATTN_WIDTH), row(KV_WIDTH), row(KV_WIDTH), row(KV_WIDTH + LANES), row(KV_WIDTH),
                   kw_spec, row(KV_WIDTH), row(CONV_CH), row(GATE_PAD)],
        out_shape=[shp(ATTN_WIDTH, BF16), shp(KV_WIDTH, F32), shp(KV_WIDTH, F32),
                   shp(KV_WIDTH + LANES, BF16), shp(KV_WIDTH, BF16), kw_shape, shp(KV_WIDTH, BF16),
                   shp(CONV_CH, F32), shp(GATE_PAD, F32)],
        compiler_params=pltpu.CompilerParams(
            dimension_semantics=("parallel",), vmem_limit_bytes=VMEM_LIMIT),
        name="proj",
    )(x2, g, w, kconst_slc, kconst_win[WINDOW:])


def _cmp_kernel(kf_ref, vf_ref, pk_ref, w1k_ref, w2k_ref, pv_ref, w1v_ref, w2v_ref, ko_ref, vo_ref):
    nchunk = kf_ref.shape[1]
    row = lax.broadcasted_iota(jnp.int32, (nchunk, 1), 0)
    for f_ref, p_ref, w1_ref, w2_ref, o_ref in ((kf_ref, pk_ref, w1k_ref, w2k_ref, ko_ref),
                                               (vf_ref, pv_ref, w1v_ref, w2v_ref, vo_ref)):
        c = f_ref[0]
        ha = jnp.dot((c + p_ref[0]).astype(BF16), w1_ref[0], preferred_element_type=F32)
        hb = jnp.dot((c + p_ref[1]).astype(BF16), w1_ref[1], preferred_element_type=F32)
        hid = ha + pltpu.roll(hb, nchunk - 1, 0)
        hid = jnp.where(row < nchunk - 1, hid, 0.0)
        act = (hid * jax.nn.sigmoid(hid)).astype(BF16)
        o_ref[0] = jnp.dot(act, w2_ref[...], preferred_element_type=F32).astype(BF16)


def _compress(kf, vf, pk, w1k, w2k, pv, w1v, w2v):
    b, nchunk, width = kf.shape
    blk = pl.BlockSpec((1, nchunk, width), lambda i: (i, 0, 0))
    oblk = pl.BlockSpec((1, nchunk, KV_WIDTH), lambda i: (i, 0, 0))
    wspecs = [_const_spec(pk.shape), _const_spec(w1k.shape), _const_spec(w2k.shape)]
    return pl.pallas_call(
        _cmp_kernel,
        grid=(b,),
        in_specs=[blk, blk] + wspecs + wspecs,
        out_specs=[oblk, oblk],
        out_shape=[jax.ShapeDtypeStruct((b, nchunk, KV_WIDTH), BF16)] * 2,
        compiler_params=pltpu.CompilerParams(
            dimension_semantics=("parallel",), vmem_limit_bytes=VMEM_LIMIT),
        name="compress",
    )(kf, vf, pk, w1k, w2k, pv, w1v, w2v)


def _attn_kernel(q_ref, kc_ref, vct_ref, ks_ref, vst_ref, kw_ref, vwt_ref, gate_ref, aux_ref, ovt_ref,
                 o_ref, s0_scr, s1_scr, e0_scr, e1_scr, acc_scr, *, kchunk):
    ncmp = kc_ref.shape[1]
    nsel = ovt_ref.shape[0] - ONES_ROWS
    nqb = Q_STEP // Q_BLOCK
    cols = nqb * COLS
    q0 = pl.program_id(1) * Q_STEP
    c0 = pl.program_id(1) * nqb
    nt = (((1,), (1,)), ((), ()))
    qf = q_ref[0].astype(F32)
    col = lax.broadcasted_iota(jnp.int32, (1, cols), 1)
    t_row = q0 + (col // COLS) * Q_BLOCK + (col & (Q_BLOCK - 1))
    zeros_half = jnp.zeros((ROWS, HEAD_DIM), F32)

    qpad, aux0 = [], []
    for qb in range(nqb):
        c_f = (c0 + qb).astype(F32)
        for g in range(N_KV):
            qg = jnp.concatenate(
                [qf[qb * Q_BLOCK:(qb + 1) * Q_BLOCK, (g * GROUP + r) * HEAD_DIM:(g * GROUP + r + 1) * HEAD_DIM]
                 for r in range(GROUP)], axis=0)
            qpad.append(jnp.concatenate([qg, zeros_half] if g == 0 else [zeros_half, qg], axis=1))
            aux0.append(aux_ref[g, 0] + c_f * aux_ref[g, 1])
    qpad = jnp.concatenate(qpad, axis=0)
    aux0 = jnp.concatenate(aux0, axis=0)
    qp0 = jnp.concatenate([qpad, aux0], axis=1).astype(BF16)

    cmp_end = lax.broadcasted_iota(jnp.int32, (ncmp, 1), 0) * CMP_STRIDE + (CMP_BLOCK - 1)
    cmp_valid = cmp_end <= t_row
    has_cmp = t_row >= CMP_BLOCK - 1
    ovt = ovt_ref[...]
    s = lax.dot_general(kc_ref[0], qp0, nt, preferred_element_type=F32)
    s = jnp.where(cmp_valid, s, NEG)
    e = jnp.exp(s - jnp.max(s, axis=0, keepdims=True))
    e_hi = e.astype(BF16)
    e_lo = (e - e_hi.astype(F32)).astype(BF16)
    red = (jnp.dot(ovt, e_hi, preferred_element_type=F32)
           + jnp.dot(ovt, e_lo, preferred_element_type=F32))
    rinv = jnp.where(has_cmp, 1.0 / red[nsel:nsel + 1], 0.0)
    o_cmp = jnp.dot(vct_ref[0], e_hi, preferred_element_type=F32) * rinv
    imp_t = red[:nsel] * rinv
    imp = []
    for j in range(nqb * N_KV):
        half = imp_t[:, j * ROWS:j * ROWS + LANES] + imp_t[:, j * ROWS + LANES:(j + 1) * ROWS]
        imp.append(half + pltpu.roll(half, Q_BLOCK, 1))

    sel_lanes = nqb * LANES
    lane = lax.broadcasted_iota(jnp.int32, (1, sel_lanes), 1)
    c_lane = c0 + lane // LANES
    blk = lax.broadcasted_iota(jnp.int32, (nsel, 1), 0)
    forced = (blk == 0) | (blk == c_lane) | (blk == c_lane - 1)
    blk_valid = blk <= c_lane
    lane_g = lax.broadcasted_iota(jnp.int32, (1, LANES), 1) < Q_BLOCK
    score = jnp.concatenate([jnp.where(lane_g, imp[N_KV * qb], imp[N_KV * qb + 1]) for qb in range(nqb)], axis=1)
    score = jnp.where(blk_valid, score + jnp.where(forced, FORCE_BONUS, 0.0), -jnp.inf)
    j_loc = lax.broadcasted_iota(jnp.int32, (SUBLANES, 1), 0)
    nv = nsel // SUBLANES
    tiles = [score[v * SUBLANES:(v + 1) * SUBLANES, :] for v in range(nv)]
    rank = [jnp.zeros((SUBLANES, sel_lanes), F32) for _ in range(nv)]
    for i in range(nsel):
        row = score[i:i + 1, :]
        for v in range(nv):
            if v * SUBLANES + SUBLANES - 1 < i:
                before = jnp.where(row > tiles[v], 1.0, 0.0)
            elif v * SUBLANES > i:
                before = jnp.where(row >= tiles[v], 1.0, 0.0)
            else:
                later = jnp.where(j_loc > i - v * SUBLANES, 1.0, 0.0)
                before = jnp.where(row > tiles[v], 1.0, jnp.where(row == tiles[v], later, 0.0))
            rank[v] = rank[v] + before
    rank = jnp.concatenate(rank, axis=0)
    sel_bias = jnp.where(blk_valid, jnp.where(rank < float(min(N_SEL, nsel)), 0.0, NEG), NEG)
    sel_bias = jnp.concatenate([sel_bias, jnp.zeros((LANES - nsel, sel_lanes), F32)], axis=0).T

    gate_t = gate_ref[0].T

    w_al = pl.multiple_of(q0, LANES)
    kr_top = w_al + lax.broadcasted_iota(jnp.int32, (Q_STEP, 1), 0)
    kr_bot = kr_top + WINDOW
    s = lax.dot_general(kw_ref[0, pl.ds(w_al, WIN_SPAN), :], qp0, nt, preferred_element_type=F32)
    s = jnp.concatenate([jnp.where(kr_top > t_row, s[:Q_STEP], NEG),
                         s[Q_STEP:WINDOW],
                         jnp.where(kr_bot <= t_row + WINDOW, s[WINDOW:], NEG)], axis=0)
    e = jnp.exp(s - jnp.max(s, axis=0, keepdims=True))
    res = jnp.dot(vwt_ref[0, :, pl.ds(w_al, WIN_SPAN)], e.astype(BF16), preferred_element_type=F32)
    o_win = res[:KV_WIDTH] * (1.0 / res[KV_WIDTH:KV_WIDTH + 1])

    sel_rows = jnp.concatenate(
        [sel_bias[j * Q_BLOCK:(j + 1) * Q_BLOCK, :] for j in range(nqb * N_KV) for _ in range(GROUP)], axis=0)
    qp = jnp.concatenate([qpad, aux0 + sel_rows], axis=1).astype(BF16)
    s_bufs = (s0_scr, s1_scr)
    e_bufs = (e0_scr, e1_scr)
    n_strip = kchunk // STRIP
    n_full = q0 // kchunk

    def produce(k0, s_buf):
        s = lax.dot_general(ks_ref[0, pl.ds(k0, kchunk), :], qp, nt, preferred_element_type=F32)
        s_buf[...] = s
        return jnp.max(s, axis=0, keepdims=True)

    def soften(s_buf, e_buf, smax, m):
        m_new = jnp.maximum(m, smax)
        for j in range(n_strip):
            rows = slice(j * STRIP, (j + 1) * STRIP)
            e_buf[rows, :] = jnp.exp(s_buf[rows, :] - m_new).astype(BF16)
        return m_new, jnp.exp(m - m_new)

    def add_values(k0, e_buf, alpha):
        vb = vst_ref[0, :, pl.ds(k0, kchunk)]
        acc_scr[...] = alpha * acc_scr[...] + jnp.dot(vb, e_buf[...], preferred_element_type=F32)

    def step(j, carry, cur, nxt):
        smax, m, alpha_prev = carry
        k0 = pl.multiple_of(j * kchunk, kchunk)
        add_values(pl.multiple_of(jnp.maximum(k0 - kchunk, 0), kchunk), e_bufs[nxt], alpha_prev)
        m, alpha = soften(s_bufs[cur], e_bufs[cur], smax, m)
        return produce(pl.multiple_of(k0 + kchunk, kchunk), s_bufs[nxt]), m, alpha

    def start(cur, nxt):
        e_bufs[nxt][...] = jnp.zeros(e_bufs[nxt].shape, BF16)
        return produce(0, s_bufs[cur])

    def slc_body(j, carry):
        return lax.cond(((n_full - j) & 1) == 0,
                        lambda cr: step(j, cr, 0, 1), lambda cr: step(j, cr, 1, 0), carry)

    acc_scr[...] = jnp.zeros(acc_scr.shape, F32)
    smax0 = lax.cond((n_full & 1) == 0, lambda: start(0, 1), lambda: start(1, 0))
    carry0 = (smax0, jnp.full((1, cols), NEG, F32), jnp.ones((1, cols), F32))
    _, m_run, alpha_prev = lax.fori_loop(0, n_full, slc_body, carry0)
    k_last = pl.multiple_of(n_full * kchunk, kchunk)
    add_values(pl.multiple_of(jnp.maximum(k_last - kchunk, 0), kchunk), e1_scr, alpha_prev)
    diag = pl.multiple_of(q0 - k_last, Q_STEP)
    causal = (q0 + lax.broadcasted_iota(jnp.int32, (Q_STEP, 1), 0)) <= t_row
    s0_scr[pl.ds(diag, Q_STEP), :] = jnp.where(causal, s0_scr[pl.ds(diag, Q_STEP), :], NEG)
    mx = s0_scr[0:STRIP, :]
    for j in range(1, n_strip):
        mx = jnp.maximum(mx, s0_scr[j * STRIP:(j + 1) * STRIP, :])
    _, alpha_last = soften(s0_scr, e0_scr, jnp.max(mx, axis=0, keepdims=True), m_run)
    add_values(k_last, e0_scr, alpha_last)
    o_slc = acc_scr[:KV_WIDTH, :] * (1.0 / acc_scr[KV_WIDTH:KV_WIDTH + 1, :])

    def gate_row(br):
        return jnp.concatenate([gate_t[h * N_BRANCH + br:h * N_BRANCH + br + 1, qb * Q_BLOCK:(qb + 1) * Q_BLOCK]
                                for qb in range(nqb) for h in range(N_HEADS)], axis=1)

    og = (gate_row(0) * o_cmp + gate_row(1) * o_slc + gate_row(2) * o_win).T
    o_ref[0] = jnp.concatenate(
        [jnp.concatenate(
            [og[qb * COLS + h * Q_BLOCK:qb * COLS + (h + 1) * Q_BLOCK,
                (h // GROUP) * HEAD_DIM:(h // GROUP + 1) * HEAD_DIM] for h in range(N_HEADS)], axis=1)
         for qb in range(nqb)], axis=0).astype(BF16)


def _attention(q, kc, vct, ks, vst, kw, vwt, gates, aux, ovt, *, kchunk):
    b, seq, _ = q.shape
    cols = Q_STEP // Q_BLOCK * COLS
    qspec = lambda width: pl.BlockSpec((1, Q_STEP, width), lambda i, j: (i, j, 0))
    full = lambda a: pl.BlockSpec((1,) + a.shape[1:], lambda i, j: (i, 0, 0))
    return pl.pallas_call(
        functools.partial(_attn_kernel, kchunk=kchunk),
        grid=(b, seq // Q_STEP),
        in_specs=[qspec(ATTN_WIDTH), full(kc), full(vct), full(ks), full(vst), full(kw), full(vwt),
                  qspec(GATE_PAD), _const_spec(aux.shape), _const_spec(ovt.shape)],
        out_specs=qspec(ATTN_WIDTH),
        out_shape=jax.ShapeDtypeStruct((b, seq, ATTN_WIDTH), BF16),
        scratch_shapes=[pltpu.VMEM((kchunk, cols), F32), pltpu.VMEM((kchunk, cols), F32),
                        pltpu.VMEM((kchunk, cols), BF16), pltpu.VMEM((kchunk, cols), BF16),
                        pltpu.VMEM((VT_ROWS, cols), F32)],
        compiler_params=pltpu.CompilerParams(
            dimension_semantics=("parallel", "arbitrary"), vmem_limit_bytes=VMEM_LIMIT),
        name="nsa_attention",
    )(q, kc, vct, ks, vst, kw, vwt, gates, aux, ovt)


def _attn_tables(seq):
    nsel = seq // SEL_BLOCK
    ncmp = seq // CMP_STRIDE
    lane = np.arange(LANES)[None, :]

    def key_consts(pos, is_pad):
        blk = (pos // SEL_BLOCK)[:, None]
        off = (pos % SEL_BLOCK)[:, None]
        live = (~is_pad)[:, None]
        t = np.where((lane == blk) & (lane < SEL_BLOCK), 1.0, 0.0)
        t = t + np.where(lane == AUX_BLK, blk, 0) + np.where(lane == AUX_OFF, off, 0)
        t = t + np.where((lane == AUX_ONE_C) | (lane == AUX_ONE_Q), 1.0, 0.0)
        return np.where(live, t, np.where(lane == AUX_PAD, 1.0, 0.0)).astype(np.float32)

    pos = np.arange(seq)
    k_slc = key_consts(pos, pos < 0)
    posw = np.arange(seq + WINDOW) - WINDOW
    k_win = key_consts(np.maximum(posw, 0), posw < 0)
    k_win = np.where(lane < SEL_BLOCK, 0.0, k_win).astype(np.float32)
    n = np.arange(ncmp)[:, None]
    k_cmp = (np.where((lane == AUX_ONE_C) | (lane == AUX_ONE_Q) | (lane == AUX_CMP_ONE), 1.0, 0.0)
             + np.where(lane == AUX_CMP_N, n, 0)).astype(np.float32)

    row = np.arange(ROWS)
    qi = (row % Q_BLOCK).astype(np.float32)[:, None]
    aux = []
    for g in range(N_KV):
        slope = np.exp2(-(g * GROUP + row // Q_BLOCK + 1).astype(np.float32))[:, None]
        t0 = (np.where(lane == AUX_BLK, slope * SEL_BLOCK, 0.0) + np.where(lane == AUX_OFF, slope, 0.0)
              + np.where(lane == AUX_ONE_Q, -slope * qi, 0.0)
              + np.where(lane == AUX_CMP_N, slope * CMP_STRIDE, 0.0)
              + np.where(lane == AUX_CMP_ONE, slope * (CMP_BLOCK - 1), 0.0)
              + np.where(lane == AUX_PAD, NEG, 0.0))
        t1 = np.where(lane == AUX_ONE_C, -slope * Q_BLOCK, 0.0)
        aux.append(np.stack([t0, t1]))
    aux = np.stack(aux).astype(np.float32)

    cmp_start = np.arange(ncmp) * CMP_STRIDE
    sel_start = np.arange(nsel) * SEL_BLOCK
    ovt = np.clip(np.minimum(cmp_start[None] + CMP_BLOCK, sel_start[:, None] + SEL_BLOCK)
                  - np.maximum(cmp_start[None], sel_start[:, None]), 0, None).astype(np.float32) / CMP_BLOCK
    ovt = np.concatenate([ovt, np.ones((ONES_ROWS, ncmp), np.float32)], axis=0)
    bf = lambda a: jnp.asarray(a, BF16)
    return bf(k_slc), bf(k_win), bf(k_cmp), jnp.asarray(aux), bf(ovt)


def _mix_out_kernel(x_ref, attn_ref, u_ref, halo_ref, cw_ref, cb_ref, lg_ref, lb_ref, wo_ref, o_ref,
                    ush_ref, cv_ref, *, rows_per_step):
    ts = u_ref.shape[1]
    span = HALO + ts
    i = pl.program_id(1)
    ush_ref[0, 0:HALO, :] = jnp.where(i > 0, halo_ref[0], 0.0)
    ush_ref[0, HALO:span, :] = u_ref[0]
    for b in range(1, SUBLANES):
        ush_ref[b, 0:span - SUBLANES, :] = ush_ref[0, b:span - SUBLANES + b, :]
    cw = cw_ref[...]
    base = HALO - (CONV_WIDTH - 1)
    for rc in range(ts // rows_per_step):
        r0 = rc * rows_per_step
        acc = jnp.zeros((rows_per_step, CONV_CH), F32) + cb_ref[...]
        for k in range(CONV_WIDTH):
            shift, start = (base + k) % SUBLANES, (base + k) // SUBLANES * SUBLANES
            acc = acc + cw[k:k + 1, :] * ush_ref[shift, r0 + start:r0 + start + rows_per_step, :]
        mu = jnp.mean(acc, axis=-1, keepdims=True)
        cen = acc - mu
        var = jnp.mean(cen * cen, axis=-1, keepdims=True)
        y = cen * lax.rsqrt(var + EPS) * lg_ref[...] + lb_ref[...]
        cv_ref[r0:r0 + rows_per_step, :] = (y * jax.nn.sigmoid(y)).astype(BF16)
    heads = jnp.concatenate([attn_ref[0], cv_ref[...]], axis=1)
    o_ref[0] = x_ref[0] + jnp.dot(heads, wo_ref[...], preferred_element_type=F32)


def _mix_out(x, attn, u, cw, cb, lg, lb, wo, *, ts):
    b, seq, d = x.shape
    blk = lambda width: pl.BlockSpec((1, ts, width), lambda bi, i: (bi, i, 0))
    halo = pl.BlockSpec((1, HALO, CONV_CH), lambda bi, i: (bi, jnp.maximum(i * (ts // HALO) - 1, 0), 0))
    return pl.pallas_call(
        functools.partial(_mix_out_kernel, rows_per_step=32),
        grid=(b, seq // ts),
        in_specs=[blk(d), blk(ATTN_WIDTH), blk(CONV_CH), halo,
                  _const_spec(cw.shape), _const_spec(cb.shape), _const_spec(lg.shape), _const_spec(lb.shape),
                  _const_spec(wo.shape)],
        out_specs=blk(d),
        out_shape=jax.ShapeDtypeStruct((b, seq, d), F32),
        scratch_shapes=[pltpu.VMEM((SUBLANES, HALO + ts, CONV_CH), F32), pltpu.VMEM((ts, CONV_CH), BF16)],
        compiler_params=pltpu.CompilerParams(
            dimension_semantics=("parallel", "arbitrary"), vmem_limit_bytes=VMEM_LIMIT),
        name="conv_outproj",
    )(x, attn, u, u, cw, cb, lg, lb, wo)


def _prep_w_in(w_in):
    o_gl = ATTN_WIDTH + 6 * KV_WIDTH
    n_gl = N_HEADS * N_BRANCH
    head = w_in[..., :o_gl]
    gl = w_in[..., o_gl:o_gl + n_gl]
    rest = w_in[..., o_gl + n_gl:]
    gl = jnp.pad(gl, ((0, 0), (0, 0), (0, GATE_PAD - n_gl)))
    return jnp.concatenate([head, rest, gl], axis=-1).astype(BF16)


def _prep_cmp(pos, w1, w2):
    nl = pos.shape[0]
    half = CMP_STRIDE
    eye = jnp.eye(N_KV, dtype=w1.dtype)
    w1r = w1.reshape(nl, 2, half, HEAD_DIM, CMP_HIDDEN)
    w1x = w1r[:, :, :, None, :, None, :] * eye[:, None, :, None]
    w1x = w1x.reshape(nl, 2, half * N_KV * HEAD_DIM, N_KV * CMP_HIDDEN).astype(BF16)
    w2x = w2[:, None, :, None, :] * eye[:, None, :, None]
    w2x = w2x.reshape(nl, N_KV * CMP_HIDDEN, N_KV * HEAD_DIM).astype(BF16)
    pr = pos.reshape(nl, 2, half, 1, HEAD_DIM)
    px = jnp.broadcast_to(pr, (nl, 2, half, N_KV, HEAD_DIM)).reshape(nl, 2, 1, half * N_KV * HEAD_DIM)
    return px, w1x, w2x


def kernel(x, ffn1_norm, ffn1_w_gate, ffn1_w_up, ffn1_w_down, mix_norm, w_in, cmp_pos_k, cmp_k_w1, cmp_k_w2,
           cmp_pos_v, cmp_v_w1, cmp_v_w2, conv_w, conv_b, conv_norm_g, conv_norm_b, w_out, ffn2_norm,
           ffn2_w_gate, ffn2_w_up, ffn2_w_down, final_norm):
    b, seq, d = x.shape
    depth = ffn1_norm.shape[0]
    n = b * seq
    tm = 512
    ts = 256
    kchunk = 512
    assert d == D_MODEL and seq % kchunk == 0 and seq % tm == 0 and seq // SEL_BLOCK <= SEL_BLOCK
    assert WINDOW % tm == 0 and kchunk % Q_STEP == 0 and Q_STEP % LANES == 0 and Q_STEP <= WINDOW

    bf = lambda w: w.astype(BF16)
    w1g, w1u, w1d = bf(ffn1_w_gate), bf(ffn1_w_up), bf(ffn1_w_down)
    w2g, w2u, w2d = bf(ffn2_w_gate), bf(ffn2_w_up), bf(ffn2_w_down)
    w_in_p = _prep_w_in(w_in)
    pk, w1k, w2k = _prep_cmp(cmp_pos_k, cmp_k_w1, cmp_k_w2)
    pv, w1v, w2v = _prep_cmp(cmp_pos_v, cmp_v_w1, cmp_v_w2)
    w_o = bf(w_out)
    row = lambda v: v.reshape(1, -1)
    fg = row(final_norm)
    kconst_slc, kconst_win, kconst_cmp, aux, ovt = _attn_tables(seq)

    def with_consts(k, consts):
        return jnp.concatenate([k, jnp.broadcast_to(consts, (b,) + consts.shape)], axis=-1)

    x2 = x.reshape(n, d)
    nchunk = seq // CMP_STRIDE
    r3 = lambda a: a.reshape(b, seq, a.shape[-1])
    tr = lambda a: jnp.swapaxes(a, 1, 2)
    tr1 = lambda a: jnp.pad(tr(a), ((0, 0), (0, ONES_ROWS), (0, 0)), constant_values=1)
    lpad = lambda a: jnp.pad(a, ((0, 0), (WINDOW, 0), (0, 0)))
    kw_pad = jnp.concatenate([jnp.zeros((WINDOW, KV_WIDTH), BF16), kconst_win[:WINDOW]], axis=1)
    kw_pad = jnp.broadcast_to(kw_pad, (b,) + kw_pad.shape)
    for l in range(depth):
        x2 = _ffn(x2, row(ffn1_norm[l]), w1g[l], w1u[l], w1d[l], fg, final=False, tm=tm)
        q, kc, vc, ks, vs, kw, vw, u, gates = _proj(x2, row(mix_norm[l]), w_in_p[l], kconst_slc, kconst_win,
                                                    batch=b, tm=tm)
        kw = lax.dynamic_update_slice(kw, kw_pad, (0, 0, 0))
        kcc, vcc = _compress(kc.reshape(b, nchunk, CMP_STRIDE * KV_WIDTH),
                             vc.reshape(b, nchunk, CMP_STRIDE * KV_WIDTH),
                             pk[l], w1k[l], w2k[l], pv[l], w1v[l], w2v[l])
        attn = _attention(r3(q), with_consts(kcc, kconst_cmp), tr(vcc),
                          r3(ks), tr1(r3(vs)), kw, tr1(lpad(r3(vw))),
                          r3(gates), aux, ovt, kchunk=kchunk)
        x3 = _mix_out(x2.reshape(b, seq, d), attn, r3(u), conv_w[l], row(conv_b[l]), row(conv_norm_g[l]),
                      row(conv_norm_b[l]), w_o[l], ts=ts)
        x2 = _ffn(x3.reshape(n, d), row(ffn2_norm[l]), w2g[l], w2u[l], w2d[l], fg,
                  final=(l == depth - 1), tm=tm)
    return x2.reshape(b, seq, d)
```

```python
import functools

import jax
import jax.numpy as jnp
import numpy as np
from jax import lax
from jax.experimental import pallas as pl
from jax.experimental.pallas import tpu as pltpu

F32 = jnp.float32
BF16 = jnp.bfloat16

D_MODEL = 1024
N_HEADS = 8
HEAD_DIM = 64
N_KV = 2
GROUP = N_HEADS // N_KV
ATTN_WIDTH = N_HEADS * HEAD_DIM
KV_WIDTH = N_KV * HEAD_DIM
N_BRANCH = 3
CMP_BLOCK = 32
CMP_STRIDE = 16
CMP_HIDDEN = 256
SEL_BLOCK = 64
N_SEL = 16
WINDOW = 512
Q_BLOCK = 64
FORCE_BONUS = 1e3
CONV_CH = D_MODEL - ATTN_WIDTH
CONV_WIDTH = 31
EPS = 1e-6
NEG = -1e30

V7X_VMEM_BYTES = 64 * 1024 * 1024
VMEM_LIMIT = V7X_VMEM_BYTES // 8 * 7
LANES = 128
SUBLANES = 8

GATE_PAD = LANES
HALO = 32
ROWS = GROUP * Q_BLOCK
COLS = N_KV * ROWS
Q_STEP = 4 * Q_BLOCK
WIN_SPAN = WINDOW + Q_STEP
STRIP = 16
ONES_ROWS = 16
VG_ROWS = HEAD_DIM + ONES_ROWS

AUX_BLK = SEL_BLOCK
AUX_OFF = SEL_BLOCK + 1
AUX_ONE_C = SEL_BLOCK + 2
AUX_ONE_Q = SEL_BLOCK + 3
AUX_CMP_N = SEL_BLOCK + 4
AUX_CMP_ONE = SEL_BLOCK + 5
AUX_PAD = SEL_BLOCK + 6


def _const_spec(shape):
    n = len(shape)
    return pl.BlockSpec(shape, lambda *_: (0,) * n, pipeline_mode=pl.Buffered(1))


def _layer_spec(stacked, layer):
    n = stacked.ndim - 1
    return pl.BlockSpec((None,) + stacked.shape[1:], lambda *_: (layer,) + (0,) * n,
                        pipeline_mode=pl.Buffered(1))


def _rms(x, g):
    ms = jnp.mean(x * x, axis=-1, keepdims=True)
    return x * lax.rsqrt(ms + EPS) * g


def _ffn_kernel(x_ref, g_ref, wg_ref, wu_ref, wd_ref, fg_ref, o_ref, *, final):
    x = x_ref[...]
    h = _rms(x, g_ref[...]).astype(BF16)
    a = jnp.dot(h, wg_ref[...], preferred_element_type=F32)
    u = jnp.dot(h, wu_ref[...], preferred_element_type=F32)
    act = (a * jax.nn.sigmoid(a) * u).astype(BF16)
    y = jnp.dot(act, wd_ref[...], preferred_element_type=F32)
    o = x + 0.5 * y
    if final:
        o = _rms(o, fg_ref[...])
    o_ref[...] = o


def _ffn(x2, g, wg, wu, wd, fg, *, layer, final, tm):
    n, d = x2.shape
    return pl.pallas_call(
        functools.partial(_ffn_kernel, final=final),
        grid=(n // tm,),
        in_specs=[
            pl.BlockSpec((tm, d), lambda i: (i, 0)),
            _const_spec((1, d)),
            _layer_spec(wg, layer),
            _layer_spec(wu, layer),
            _layer_spec(wd, layer),
            _const_spec((1, d)),
        ],
        out_specs=pl.BlockSpec((tm, d), lambda i: (i, 0)),
        out_shape=jax.ShapeDtypeStruct((n, d), F32),
        compiler_params=pltpu.CompilerParams(
            dimension_semantics=("parallel",), vmem_limit_bytes=VMEM_LIMIT),
        name="ffn",
    )(x2, g, wg, wu, wd, fg)


def _proj_kernel(x_ref, g_ref, w_ref, kcs_ref, kcw_ref, q_ref, kc_ref, vc_ref, ks_ref, vs_ref, kw_ref, vw_ref,
                 u_ref, gate_ref, *, lead):
    j = pl.program_id(1)

    @pl.when(j < lead)
    def _():
        kw_ref[0, :, :KV_WIDTH] = jnp.zeros((kw_ref.shape[1], KV_WIDTH), BF16)
        kw_ref[0, :, KV_WIDTH:] = kcw_ref[...]
        vw_ref[0] = jnp.zeros(vw_ref.shape[1:], BF16)

    @pl.when(j >= lead)
    def _():
        h = _rms(x_ref[...], g_ref[...]).astype(BF16)
        z = jnp.dot(h, w_ref[...], preferred_element_type=F32)
        o = 0
        q_ref[...] = (z[:, o:o + ATTN_WIDTH] * (HEAD_DIM ** -0.5)).astype(BF16)
        o += ATTN_WIDTH
        kc_ref[...] = z[:, o:o + KV_WIDTH]
        o += KV_WIDTH
        vc_ref[...] = z[:, o:o + KV_WIDTH]
        o += KV_WIDTH
        ks_ref[:, :KV_WIDTH] = z[:, o:o + KV_WIDTH].astype(BF16)
        ks_ref[:, KV_WIDTH:] = kcs_ref[...]
        o += KV_WIDTH
        vs_ref[...] = z[:, o:o + KV_WIDTH].astype(BF16)
        o += KV_WIDTH
        kw_ref[0, :, :KV_WIDTH] = z[:, o:o + KV_WIDTH].astype(BF16)
        kw_ref[0, :, KV_WIDTH:] = kcw_ref[...]
        o += KV_WIDTH
        vw_ref[0] = z[:, o:o + KV_WIDTH].astype(BF16)
        o += KV_WIDTH
        ga = z[:, o:o + CONV_CH]
        o += CONV_CH
        gb = z[:, o:o + CONV_CH]
        o += CONV_CH
        u_ref[...] = ga * jax.nn.sigmoid(gb)
        gate_ref[...] = jax.nn.sigmoid(z[:, o:o + GATE_PAD])


def _proj(x2, g, w, kconst_slc, kconst_win, *, layer, batch, tm):
    n, d = x2.shape
    seq = n // batch
    tiles = seq // tm
    lead = WINDOW // tm
    tile = lambda bi, j: bi * tiles + jnp.maximum(j - lead, 0)
    row = lambda width: pl.BlockSpec((tm, width), lambda bi, j: (tile(bi, j), 0))
    kconst_s = pl.BlockSpec((tm, LANES), lambda bi, j: (jnp.maximum(j - lead, 0), 0))
    kconst_w = pl.BlockSpec((tm, LANES), lambda bi, j: (j, 0))
    shp = lambda width, dt: jax.ShapeDtypeStruct((n, width), dt)
    padded = lambda width: pl.BlockSpec((1, tm, width), lambda bi, j: (bi, j, 0))
    pshape = lambda width: jax.ShapeDtypeStruct((batch, WINDOW + seq, width), BF16)
    return pl.pallas_call(
        functools.partial(_proj_kernel, lead=lead),
        grid=(batch, tiles + lead),
        in_specs=[row(d), _const_spec((1, d)), _layer_spec(w, layer), kconst_s, kconst_w],
        out_specs=[row(ATTN_WIDTH), row(KV_WIDTH), row(KV_WIDTH), row(KV_WIDTH + LANES), row(KV_WIDTH),
                   padded(KV_WIDTH + LANES), padded(KV_WIDTH), row(CONV_CH), row(GATE_PAD)],
        out_shape=[shp(ATTN_WIDTH, BF16), shp(KV_WIDTH, F32), shp(KV_WIDTH, F32),
                   shp(KV_WIDTH + LANES, BF16), shp(KV_WIDTH, BF16), pshape(KV_WIDTH + LANES), pshape(KV_WIDTH),
                   shp(CONV_CH, F32), shp(GATE_PAD, F32)],
        compiler_params=pltpu.CompilerParams(
            dimension_semantics=("parallel", "arbitrary"), vmem_limit_bytes=VMEM_LIMIT),
        name="proj",
    )(x2, g, w, kconst_slc, kconst_win)


def _cmp_kernel(kf_ref, vf_ref, pk_ref, w1k_ref, w2k_ref, pv_ref, w1v_ref, w2v_ref, ko_ref, vo_ref):
    nchunk = kf_ref.shape[1] // CMP_STRIDE
    row = lax.broadcasted_iota(jnp.int32, (nchunk, 1), 0)
    for f_ref, p_ref, w1_ref, w2_ref, o_ref in ((kf_ref, pk_ref, w1k_ref, w2k_ref, ko_ref),
                                               (vf_ref, pv_ref, w1v_ref, w2v_ref, vo_ref)):
        ha = jnp.zeros((nchunk, N_KV * CMP_HIDDEN), F32)
        hb = jnp.zeros((nchunk, N_KV * CMP_HIDDEN), F32)
        for l in range(CMP_STRIDE):
            tok = f_ref[0, pl.ds(l, nchunk, stride=CMP_STRIDE), :]
            ha = ha + jnp.dot((tok + p_ref[0, l]).astype(BF16), w1_ref[0, l], preferred_element_type=F32)
            hb = hb + jnp.dot((tok + p_ref[1, l]).astype(BF16), w1_ref[1, l], preferred_element_type=F32)
        hid = ha + pltpu.roll(hb, nchunk - 1, 0)
        hid = jnp.where(row < nchunk - 1, hid, 0.0)
        act = (hid * jax.nn.sigmoid(hid)).astype(BF16)
        o_ref[0] = jnp.dot(act, w2_ref[...], preferred_element_type=F32).astype(BF16)


def _compress(kf, vf, pk, w1k, w2k, pv, w1v, w2v, *, layer):
    b, seq, width = kf.shape
    nchunk = seq // CMP_STRIDE
    blk = pl.BlockSpec((1, seq, width), lambda i: (i, 0, 0))
    oblk = pl.BlockSpec((1, nchunk, KV_WIDTH), lambda i: (i, 0, 0))
    wspecs = [_layer_spec(w, layer) for w in (pk, w1k, w2k, pv, w1v, w2v)]
    return pl.pallas_call(
        _cmp_kernel,
        grid=(b,),
        in_specs=[blk, blk] + wspecs,
        out_specs=[oblk, oblk],
        out_shape=[jax.ShapeDtypeStruct((b, nchunk, KV_WIDTH), BF16)] * 2,
        compiler_params=pltpu.CompilerParams(
            dimension_semantics=("parallel",), vmem_limit_bytes=VMEM_LIMIT),
        name="compress",
    )(kf, vf, pk, w1k, w2k, pv, w1v, w2v)


def _attn_kernel(q_ref, kc_ref, vct_ref, ks_ref, vst_ref, kw_ref, vwt_ref, gate_ref, aux_ref, ovt_ref,
                 o_ref, s0_scr, s1_scr, e0_scr, e1_scr, acc_scr, *, kchunk):
    ncmp = kc_ref.shape[1]
    nsel = ovt_ref.shape[0] - ONES_ROWS
    nqb = Q_STEP // Q_BLOCK
    cols = nqb * COLS
    q0 = pl.program_id(1) * Q_STEP
    c0 = pl.program_id(1) * nqb
    nt = (((1,), (1,)), ((), ()))
    qf = q_ref[0].astype(F32)
    col = lax.broadcasted_iota(jnp.int32, (1, cols), 1)
    t_row = q0 + (col // COLS) * Q_BLOCK + (col & (Q_BLOCK - 1))
    zeros_half = jnp.zeros((ROWS, HEAD_DIM), F32)

    def by_group(fn):
        return jnp.concatenate([fn(jb % N_KV, slice(jb * ROWS, (jb + 1) * ROWS)) for jb in range(cols // ROWS)],
                               axis=1)

    qpad, aux0 = [], []
    for qb in range(nqb):
        c_f = (c0 + qb).astype(F32)
        for g in range(N_KV):
            qg = jnp.concatenate(
                [qf[qb * Q_BLOCK:(qb + 1) * Q_BLOCK, (g * GROUP + r) * HEAD_DIM:(g * GROUP + r + 1) * HEAD_DIM]
                 for r in range(GROUP)], axis=0)
            qpad.append(jnp.concatenate([qg, zeros_half] if g == 0 else [zeros_half, qg], axis=1))
            aux0.append(aux_ref[g, 0] + c_f * aux_ref[g, 1])
    qpad = jnp.concatenate(qpad, axis=0)
    aux0 = jnp.concatenate(aux0, axis=0)
    qp0 = jnp.concatenate([qpad, aux0], axis=1).astype(BF16)

    cmp_end = lax.broadcasted_iota(jnp.int32, (ncmp, 1), 0) * CMP_STRIDE + (CMP_BLOCK - 1)
    cmp_valid = cmp_end <= t_row
    has_cmp = t_row >= CMP_BLOCK - 1
    ovt = ovt_ref[...]
    s = lax.dot_general(kc_ref[0], qp0, nt, preferred_element_type=F32)
    s = jnp.where(cmp_valid, s, NEG)
    e = jnp.exp(s - jnp.max(s, axis=0, keepdims=True))
    e_hi = e.astype(BF16)
    e_lo = (e - e_hi.astype(F32)).astype(BF16)
    red = (jnp.dot(ovt, e_hi, preferred_element_type=F32)
           + jnp.dot(ovt, e_lo, preferred_element_type=F32))
    rinv = jnp.where(has_cmp, 1.0 / red[nsel:nsel + 1], 0.0)
    o_cmp = by_group(lambda g, cb: jnp.dot(vct_ref[0, g], e_hi[:, cb], preferred_element_type=F32)) * rinv
    imp_t = red[:nsel] * rinv
    imp = []
    for j in range(nqb * N_KV):
        half = imp_t[:, j * ROWS:j * ROWS + LANES] + imp_t[:, j * ROWS + LANES:(j + 1) * ROWS]
        imp.append(half + pltpu.roll(half, Q_BLOCK, 1))

    sel_lanes = nqb * LANES
    lane = lax.broadcasted_iota(jnp.int32, (1, sel_lanes), 1)
    c_lane = c0 + lane // LANES
    blk = lax.broadcasted_iota(jnp.int32, (nsel, 1), 0)
    forced = (blk == 0) | (blk == c_lane) | (blk == c_lane - 1)
    blk_valid = blk <= c_lane
    lane_g = lax.broadcasted_iota(jnp.int32, (1, LANES), 1) < Q_BLOCK
    score = jnp.concatenate([jnp.where(lane_g, imp[N_KV * qb], imp[N_KV * qb + 1]) for qb in range(nqb)], axis=1)
    score = jnp.where(blk_valid, score + jnp.where(forced, FORCE_BONUS, 0.0), -jnp.inf)
    j_loc = lax.broadcasted_iota(jnp.int32, (SUBLANES, 1), 0)
    nv = nsel // SUBLANES
    tiles = [score[v * SUBLANES:(v + 1) * SUBLANES, :] for v in range(nv)]
    rank = [jnp.zeros((SUBLANES, sel_lanes), F32) for _ in range(nv)]
    for i in range(nsel):
        row = score[i:i + 1, :]
        for v in range(nv):
            if v * SUBLANES + SUBLANES - 1 < i:
                before = jnp.where(row > tiles[v], 1.0, 0.0)
            elif v * SUBLANES > i:
                before = jnp.where(row >= tiles[v], 1.0, 0.0)
            else:
                later = jnp.where(j_loc > i - v * SUBLANES, 1.0, 0.0)
                before = jnp.where(row > tiles[v], 1.0, jnp.where(row == tiles[v], later, 0.0))
            rank[v] = rank[v] + before
    rank = jnp.concatenate(rank, axis=0)
    sel_bias = jnp.where(blk_valid, jnp.where(rank < float(min(N_SEL, nsel)), 0.0, NEG), NEG)
    sel_bias = jnp.concatenate([sel_bias, jnp.zeros((LANES - nsel, sel_lanes), F32)], axis=0).T

    gate_t = gate_ref[0].T

    w_al = pl.multiple_of(q0, LANES)
    kr_top = w_al + lax.broadcasted_iota(jnp.int32, (Q_STEP, 1), 0)
    kr_bot = kr_top + WINDOW
    s = lax.dot_general(kw_ref[0, pl.ds(w_al, WIN_SPAN), :], qp0, nt, preferred_element_type=F32)
    s = jnp.concatenate([jnp.where(kr_top > t_row, s[:Q_STEP], NEG),
                         s[Q_STEP:WINDOW],
                         jnp.where(kr_bot <= t_row + WINDOW, s[WINDOW:], NEG)], axis=0)
    e = jnp.exp(s - jnp.max(s, axis=0, keepdims=True))
    e = e.astype(BF16)
    res = by_group(lambda g, cb: jnp.dot(vwt_ref[0, g, :, pl.ds(w_al, WIN_SPAN)], e[:, cb],
                                         preferred_element_type=F32))
    o_win = res[:HEAD_DIM] * (1.0 / res[HEAD_DIM:HEAD_DIM + 1])

    sel_rows = jnp.concatenate(
        [sel_bias[j * Q_BLOCK:(j + 1) * Q_BLOCK, :] for j in range(nqb * N_KV) for _ in range(GROUP)], axis=0)
    qp = jnp.concatenate([qpad, aux0 + sel_rows], axis=1).astype(BF16)
    s_bufs = (s0_scr, s1_scr)
    e_bufs = (e0_scr, e1_scr)
    n_strip = kchunk // STRIP
    n_full = q0 // kchunk

    def produce(k0, s_buf):
        s = lax.dot_general(ks_ref[0, pl.ds(k0, kchunk), :], qp, nt, preferred_element_type=F32)
        s_buf[...] = s
        return jnp.max(s, axis=0, keepdims=True)

    def soften(s_buf, e_buf, smax, m):
        m_new = jnp.maximum(m, smax)
        for cb in range(cols // COLS):
            cs = slice(cb * COLS, (cb + 1) * COLS)
            m_cb = m_new[:, cs]
            for j in range(n_strip):
                rows = slice(j * STRIP, (j + 1) * STRIP)
                e_buf[rows, cs] = jnp.exp(s_buf[rows, cs] - m_cb).astype(BF16)
        return m_new, jnp.exp(m - m_new)

    def add_values(k0, e_buf, alpha):
        for jb in range(cols // ROWS):
            cb = slice(jb * ROWS, (jb + 1) * ROWS)
            vb = vst_ref[0, jb % N_KV, :, pl.ds(k0, kchunk)]
            acc_scr[:, cb] = alpha[:, cb] * acc_scr[:, cb] + jnp.dot(vb, e_buf[:, cb], preferred_element_type=F32)

    def step(j, carry, cur, nxt):
        smax, m, alpha_prev = carry
        k0 = pl.multiple_of(j * kchunk, kchunk)
        add_values(pl.multiple_of(jnp.maximum(k0 - kchunk, 0), kchunk), e_bufs[nxt], alpha_prev)
        m, alpha = soften(s_bufs[cur], e_bufs[cur], smax, m)
        return produce(pl.multiple_of(k0 + kchunk, kchunk), s_bufs[nxt]), m, alpha

    def start(cur, nxt):
        e_bufs[nxt][...] = jnp.zeros(e_bufs[nxt].shape, BF16)
        return produce(0, s_bufs[cur])

    def slc_body(j, carry):
        return lax.cond(((n_full - j) & 1) == 0,
                        lambda cr: step(j, cr, 0, 1), lambda cr: step(j, cr, 1, 0), carry)

    acc_scr[...] = jnp.zeros(acc_scr.shape, F32)
    smax0 = lax.cond((n_full & 1) == 0, lambda: start(0, 1), lambda: start(1, 0))
    carry0 = (smax0, jnp.full((1, cols), NEG, F32), jnp.ones((1, cols), F32))
    _, m_run, alpha_prev = lax.fori_loop(0, n_full, slc_body, carry0)
    k_last = pl.multiple_of(n_full * kchunk, kchunk)
    add_values(pl.multiple_of(jnp.maximum(k_last - kchunk, 0), kchunk), e1_scr, alpha_prev)
    diag = pl.multiple_of(q0 - k_last, Q_STEP)
    causal = (q0 + lax.broadcasted_iota(jnp.int32, (Q_STEP, 1), 0)) <= t_row
    s0_scr[pl.ds(diag, Q_STEP), :] = jnp.where(causal, s0_scr[pl.ds(diag, Q_STEP), :], NEG)
    mx = s0_scr[0:STRIP, :]
    for j in range(1, n_strip):
        mx = jnp.maximum(mx, s0_scr[j * STRIP:(j + 1) * STRIP, :])
    _, alpha_last = soften(s0_scr, e0_scr, jnp.max(mx, axis=0, keepdims=True), m_run)
    add_values(k_last, e0_scr, alpha_last)
    o_slc = acc_scr[:HEAD_DIM, :] * (1.0 / acc_scr[HEAD_DIM:HEAD_DIM + 1, :])

    def gate_row(br):
        return jnp.concatenate([gate_t[h * N_BRANCH + br:h * N_BRANCH + br + 1, qb * Q_BLOCK:(qb + 1) * Q_BLOCK]
                                for qb in range(nqb) for h in range(N_HEADS)], axis=1)

    og = gate_row(0) * o_cmp + gate_row(1) * o_slc + gate_row(2) * o_win
    og = jnp.concatenate([og, jnp.zeros((LANES - HEAD_DIM, cols), F32)], axis=0).T
    o_ref[0] = jnp.concatenate(
        [jnp.concatenate(
            [og[qb * COLS + h * Q_BLOCK:qb * COLS + (h + 1) * Q_BLOCK, :HEAD_DIM] for h in range(N_HEADS)], axis=1)
         for qb in range(nqb)], axis=0).astype(BF16)


def _attention(q, kc, vct, ks, vst, kw, vwt, gates, aux, ovt, *, kchunk):
    b, seq, _ = q.shape
    cols = Q_STEP // Q_BLOCK * COLS
    qspec = lambda width: pl.BlockSpec((1, Q_STEP, width), lambda i, j: (i, j, 0))
    full = lambda a: pl.BlockSpec((1,) + a.shape[1:], lambda i, j: (i,) + (0,) * (a.ndim - 1))
    return pl.pallas_call(
        functools.partial(_attn_kernel, kchunk=kchunk),
        grid=(b, seq // Q_STEP),
        in_specs=[qspec(ATTN_WIDTH), full(kc), full(vct), full(ks), full(vst), full(kw), full(vwt),
                  qspec(GATE_PAD), _const_spec(aux.shape), _const_spec(ovt.shape)],
        out_specs=qspec(ATTN_WIDTH),
        out_shape=jax.ShapeDtypeStruct((b, seq, ATTN_WIDTH), BF16),
        scratch_shapes=[pltpu.VMEM((kchunk, cols), F32), pltpu.VMEM((kchunk, cols), F32),
                        pltpu.VMEM((kchunk, cols), BF16), pltpu.VMEM((kchunk, cols), BF16),
                        pltpu.VMEM((VG_ROWS, cols), F32)],
        compiler_params=pltpu.CompilerParams(
            dimension_semantics=("parallel", "arbitrary"), vmem_limit_bytes=VMEM_LIMIT),
        name="nsa_attention",
    )(q, kc, vct, ks, vst, kw, vwt, gates, aux, ovt)


def _attn_tables(seq):
    nsel = seq // SEL_BLOCK
    ncmp = seq // CMP_STRIDE
    lane = np.arange(LANES)[None, :]

    def key_consts(pos, is_pad):
        blk = (pos // SEL_BLOCK)[:, None]
        off = (pos % SEL_BLOCK)[:, None]
        live = (~is_pad)[:, None]
        t = np.where((lane == blk) & (lane < SEL_BLOCK), 1.0, 0.0)
        t = t + np.where(lane == AUX_BLK, blk, 0) + np.where(lane == AUX_OFF, off, 0)
        t = t + np.where((lane == AUX_ONE_C) | (lane == AUX_ONE_Q), 1.0, 0.0)
        return np.where(live, t, np.where(lane == AUX_PAD, 1.0, 0.0)).astype(np.float32)

    pos = np.arange(seq)
    k_slc = key_consts(pos, pos < 0)
    posw = np.arange(seq + WINDOW) - WINDOW
    k_win = key_consts(np.maximum(posw, 0), posw < 0)
    k_win = np.where(lane < SEL_BLOCK, 0.0, k_win).astype(np.float32)
    n = np.arange(ncmp)[:, None]
    k_cmp = (np.where((lane == AUX_ONE_C) | (lane == AUX_ONE_Q) | (lane == AUX_CMP_ONE), 1.0, 0.0)
             + np.where(lane == AUX_CMP_N, n, 0)).astype(np.float32)

    row = np.arange(ROWS)
    qi = (row % Q_BLOCK).astype(np.float32)[:, None]
    aux = []
    for g in range(N_KV):
        slope = np.exp2(-(g * GROUP + row // Q_BLOCK + 1).astype(np.float32))[:, None]
        t0 = (np.where(lane == AUX_BLK, slope * SEL_BLOCK, 0.0) + np.where(lane == AUX_OFF, slope, 0.0)
              + np.where(lane == AUX_ONE_Q, -slope * qi, 0.0)
              + np.where(lane == AUX_CMP_N, slope * CMP_STRIDE, 0.0)
              + np.where(lane == AUX_CMP_ONE, slope * (CMP_BLOCK - 1), 0.0)
              + np.where(lane == AUX_PAD, NEG, 0.0))
        t1 = np.where(lane == AUX_ONE_C, -slope * Q_BLOCK, 0.0)
        aux.append(np.stack([t0, t1]))
    aux = np.stack(aux).astype(np.float32)

    cmp_start = np.arange(ncmp) * CMP_STRIDE
    sel_start = np.arange(nsel) * SEL_BLOCK
    ovt = np.clip(np.minimum(cmp_start[None] + CMP_BLOCK, sel_start[:, None] + SEL_BLOCK)
                  - np.maximum(cmp_start[None], sel_start[:, None]), 0, None).astype(np.float32) / CMP_BLOCK
    ovt = np.concatenate([ovt, np.ones((ONES_ROWS, ncmp), np.float32)], axis=0)
    bf = lambda a: jnp.asarray(a, BF16)
    return bf(k_slc), bf(k_win), bf(k_cmp), jnp.asarray(aux), bf(ovt)


def _mix_out_kernel(x_ref, attn_ref, u_ref, halo_ref, cw_ref, cb_ref, lg_ref, lb_ref, wo_ref, o_ref,
                    ush_ref, cv_ref, *, rows_per_step):
    ts = u_ref.shape[1]
    span = HALO + ts
    i = pl.program_id(1)
    ush_ref[0, 0:HALO, :] = jnp.where(i > 0, halo_ref[0], 0.0)
    ush_ref[0, HALO:span, :] = u_ref[0]
    for b in range(1, SUBLANES):
        ush_ref[b, 0:span - SUBLANES, :] = ush_ref[0, b:span - SUBLANES + b, :]
    cw = cw_ref[...]
    base = HALO - (CONV_WIDTH - 1)
    for rc in range(ts // rows_per_step):
        r0 = rc * rows_per_step
        acc = jnp.zeros((rows_per_step, CONV_CH), F32) + cb_ref[...]
        for k in range(CONV_WIDTH):
            shift, start = (base + k) % SUBLANES, (base + k) // SUBLANES * SUBLANES
            acc = acc + cw[k:k + 1, :] * ush_ref[shift, r0 + start:r0 + start + rows_per_step, :]
        mu = jnp.mean(acc, axis=-1, keepdims=True)
        cen = acc - mu
        var = jnp.mean(cen * cen, axis=-1, keepdims=True)
        y = cen * lax.rsqrt(var + EPS) * lg_ref[...] + lb_ref[...]
        cv_ref[r0:r0 + rows_per_step, :] = (y * jax.nn.sigmoid(y)).astype(BF16)
    heads = jnp.concatenate([attn_ref[0], cv_ref[...]], axis=1)
    o_ref[0] = x_ref[0] + jnp.dot(heads, wo_ref[...], preferred_element_type=F32)


def _mix_out(x, attn, u, cw, cb, lg, lb, wo, *, layer, ts):
    b, seq, d = x.shape
    blk = lambda width: pl.BlockSpec((1, ts, width), lambda bi, i: (bi, i, 0))
    halo = pl.BlockSpec((1, HALO, CONV_CH), lambda bi, i: (bi, jnp.maximum(i * (ts // HALO) - 1, 0), 0))
    return pl.pallas_call(
        functools.partial(_mix_out_kernel, rows_per_step=32),
        grid=(b, seq // ts),
        in_specs=[blk(d), blk(ATTN_WIDTH), blk(CONV_CH), halo,
                  _const_spec(cw.shape), _const_spec(cb.shape), _const_spec(lg.shape), _const_spec(lb.shape),
                  _layer_spec(wo, layer)],
        out_specs=blk(d),
        out_shape=jax.ShapeDtypeStruct((b, seq, d), F32),
        scratch_shapes=[pltpu.VMEM((SUBLANES, HALO + ts, CONV_CH), F32), pltpu.VMEM((ts, CONV_CH), BF16)],
        compiler_params=pltpu.CompilerParams(
            dimension_semantics=("parallel", "arbitrary"), vmem_limit_bytes=VMEM_LIMIT),
        name="conv_outproj",
    )(x, attn, u, u, cw, cb, lg, lb, wo)


def _prep_w_in(w_in):
    o_gl = ATTN_WIDTH + 6 * KV_WIDTH
    n_gl = N_HEADS * N_BRANCH
    head = w_in[..., :o_gl]
    gl = w_in[..., o_gl:o_gl + n_gl]
    rest = w_in[..., o_gl + n_gl:]
    gl = jnp.pad(gl, ((0, 0), (0, 0), (0, GATE_PAD - n_gl)))
    return jnp.concatenate([head, rest, gl], axis=-1).astype(BF16)


def _prep_cmp(pos, w1, w2):
    nl = pos.shape[0]
    half = CMP_STRIDE
    eye = jnp.eye(N_KV, dtype=w1.dtype)
    w1r = w1.reshape(nl, 2, half, HEAD_DIM, CMP_HIDDEN)
    w1x = w1r[:, :, :, None, :, None, :] * eye[:, None, :, None]
    w1x = w1x.reshape(nl, 2, half, N_KV * HEAD_DIM, N_KV * CMP_HIDDEN).astype(BF16)
    w2x = w2[:, None, :, None, :] * eye[:, None, :, None]
    w2x = w2x.reshape(nl, N_KV * CMP_HIDDEN, N_KV * HEAD_DIM).astype(BF16)
    pr = pos.reshape(nl, 2, half, 1, 1, HEAD_DIM)
    px = jnp.broadcast_to(pr, (nl, 2, half, 1, N_KV, HEAD_DIM)).reshape(nl, 2, half, 1, N_KV * HEAD_DIM)
    return px, w1x, w2x


def kernel(x, ffn1_norm, ffn1_w_gate, ffn1_w_up, ffn1_w_down, mix_norm, w_in, cmp_pos_k, cmp_k_w1, cmp_k_w2,
           cmp_pos_v, cmp_v_w1, cmp_v_w2, conv_w, conv_b, conv_norm_g, conv_norm_b, w_out, ffn2_norm,
           ffn2_w_gate, ffn2_w_up, ffn2_w_down, final_norm):
    b, seq, d = x.shape
    depth = ffn1_norm.shape[0]
    n = b * seq
    tm = 512
    ts = 512
    kchunk = 512
    assert d == D_MODEL and seq % kchunk == 0 and seq % tm == 0 and seq // SEL_BLOCK <= SEL_BLOCK
    assert WINDOW % tm == 0 and kchunk % Q_STEP == 0 and Q_STEP % LANES == 0 and Q_STEP <= WINDOW

    bf = lambda w: w.astype(BF16)
    w1g, w1u, w1d = bf(ffn1_w_gate), bf(ffn1_w_up), bf(ffn1_w_down)
    w2g, w2u, w2d = bf(ffn2_w_gate), bf(ffn2_w_up), bf(ffn2_w_down)
    w_in_p = _prep_w_in(w_in)
    pk, w1k, w2k = _prep_cmp(cmp_pos_k, cmp_k_w1, cmp_k_w2)
    pv, w1v, w2v = _prep_cmp(cmp_pos_v, cmp_v_w1, cmp_v_w2)
    w_o = bf(w_out)
    row = lambda v: v.reshape(1, -1)
    fg = row(final_norm)
    kconst_slc, kconst_win, kconst_cmp, aux, ovt = _attn_tables(seq)

    def with_consts(k, consts):
        return jnp.concatenate([k, jnp.broadcast_to(consts, (b,) + consts.shape)], axis=-1)

    x2 = x.reshape(n, d)
    r3 = lambda a: a.reshape(b, seq, a.shape[-1])
    tr = lambda a: jnp.swapaxes(a, 1, 2).reshape(b, N_KV, HEAD_DIM, a.shape[1])
    tr1 = lambda a: jnp.pad(tr(a), ((0, 0), (0, 0), (0, ONES_ROWS), (0, 0)), constant_values=1)
    for l in range(depth):
        x2 = _ffn(x2, row(ffn1_norm[l]), w1g, w1u, w1d, fg, layer=l, final=False, tm=tm)
        q, kc, vc, ks, vs, kw, vw, u, gates = _proj(x2, row(mix_norm[l]), w_in_p, kconst_slc, kconst_win,
                                                    layer=l, batch=b, tm=tm)
        kcc, vcc = _compress(r3(kc), r3(vc), pk, w1k, w2k, pv, w1v, w2v, layer=l)
        attn = _attention(r3(q), with_consts(kcc, kconst_cmp), tr(vcc),
                          r3(ks), tr1(r3(vs)), kw, tr1(vw),
                          r3(gates), aux, ovt, kchunk=kchunk)
        x3 = _mix_out(x2.reshape(b, seq, d), attn, r3(u), conv_w[l], row(conv_b[l]), row(conv_norm_g[l]),
                      row(conv_norm_b[l]), w_o, layer=l, ts=ts)
        x2 = _ffn(x3.reshape(n, d), row(ffn2_norm[l]), w2g, w2u, w2d, fg, layer=l,
                  final=(l == depth - 1), tm=tm)
    return x2.reshape(b, seq, d)
```

```python
import functools

import jax
import jax.numpy as jnp
import numpy as np
from jax import lax
from jax.experimental import pallas as pl
from jax.experimental.pallas import tpu as pltpu

F32 = jnp.float32
BF16 = jnp.bfloat16

D_MODEL = 1024
N_HEADS = 8
HEAD_DIM = 64
N_KV = 2
GROUP = N_HEADS // N_KV
ATTN_WIDTH = N_HEADS * HEAD_DIM
KV_WIDTH = N_KV * HEAD_DIM
N_BRANCH = 3
CMP_BLOCK = 32
CMP_STRIDE = 16
CMP_HIDDEN = 256
SEL_BLOCK = 64
N_SEL = 16
WINDOW = 512
Q_BLOCK = 64
FORCE_BONUS = 1e3
CONV_CH = D_MODEL - ATTN_WIDTH
CONV_WIDTH = 31
EPS = 1e-6
NEG = -1e30

V7X_VMEM_BYTES = 64 * 1024 * 1024
VMEM_LIMIT = V7X_VMEM_BYTES // 8 * 7
LANES = 128
SUBLANES = 8

GATE_PAD = LANES
HALO = 32
ROWS = GROUP * Q_BLOCK
COLS = N_KV * ROWS
Q_STEP = 4 * Q_BLOCK
WIN_SPAN = WINDOW + Q_STEP
STRIP = 16
ONES_ROWS = 16
VG_ROWS = HEAD_DIM + ONES_ROWS

AUX_BLK = SEL_BLOCK
AUX_OFF = SEL_BLOCK + 1
AUX_ONE_C = SEL_BLOCK + 2
AUX_ONE_Q = SEL_BLOCK + 3
AUX_CMP_N = SEL_BLOCK + 4
AUX_CMP_ONE = SEL_BLOCK + 5
AUX_PAD = SEL_BLOCK + 6


def _const_spec(shape):
    n = len(shape)
    return pl.BlockSpec(shape, lambda *_: (0,) * n, pipeline_mode=pl.Buffered(1))


def _layer_spec(stacked, layer):
    n = stacked.ndim - 1
    return pl.BlockSpec((None,) + stacked.shape[1:], lambda *_: (layer,) + (0,) * n,
                        pipeline_mode=pl.Buffered(1))


def _rms(x, g):
    ms = jnp.mean(x * x, axis=-1, keepdims=True)
    return x * lax.rsqrt(ms + EPS) * g


def _ffn_kernel(x_ref, g_ref, wg_ref, wu_ref, wd_ref, fg_ref, o_ref, *, final):
    x = x_ref[...]
    h = _rms(x, g_ref[...]).astype(BF16)
    a = jnp.dot(h, wg_ref[...], preferred_element_type=F32)
    u = jnp.dot(h, wu_ref[...], preferred_element_type=F32)
    act = (a * jax.nn.sigmoid(a) * u).astype(BF16)
    y = jnp.dot(act, wd_ref[...], preferred_element_type=F32)
    o = x + 0.5 * y
    if final:
        o = _rms(o, fg_ref[...])
    o_ref[...] = o


def _ffn(x2, g, wg, wu, wd, fg, *, layer, final, tm):
    n, d = x2.shape
    return pl.pallas_call(
        functools.partial(_ffn_kernel, final=final),
        grid=(n // tm,),
        in_specs=[
            pl.BlockSpec((tm, d), lambda i: (i, 0)),
            _const_spec((1, d)),
            _layer_spec(wg, layer),
            _layer_spec(wu, layer),
            _layer_spec(wd, layer),
            _const_spec((1, d)),
        ],
        out_specs=pl.BlockSpec((tm, d), lambda i: (i, 0)),
        out_shape=jax.ShapeDtypeStruct((n, d), F32),
        compiler_params=pltpu.CompilerParams(
            dimension_semantics=("parallel",), vmem_limit_bytes=VMEM_LIMIT),
        name="ffn",
    )(x2, g, wg, wu, wd, fg)


def _proj_kernel(x_ref, g_ref, w_ref, kcs_ref, kcw_ref, q_ref, kc_ref, vc_ref, ks_ref, vs_ref, kw_ref, vw_ref,
                 u_ref, gate_ref, *, lead):
    j = pl.program_id(1)

    @pl.when(j < lead)
    def _():
        kw_ref[0, :, :KV_WIDTH] = jnp.zeros((kw_ref.shape[1], KV_WIDTH), BF16)
        kw_ref[0, :, KV_WIDTH:] = kcw_ref[...]
        vw_ref[0] = jnp.zeros(vw_ref.shape[1:], BF16)

    @pl.when(j >= lead)
    def _():
        h = _rms(x_ref[...], g_ref[...]).astype(BF16)
        z = jnp.dot(h, w_ref[...], preferred_element_type=F32)
        o = 0
        q_ref[...] = (z[:, o:o + ATTN_WIDTH] * (HEAD_DIM ** -0.5)).astype(BF16)
        o += ATTN_WIDTH
        kc_ref[...] = z[:, o:o + KV_WIDTH]
        o += KV_WIDTH
        vc_ref[...] = z[:, o:o + KV_WIDTH]
        o += KV_WIDTH
        ks_ref[:, :KV_WIDTH] = z[:, o:o + KV_WIDTH].astype(BF16)
        ks_ref[:, KV_WIDTH:] = kcs_ref[...]
        o += KV_WIDTH
        vs_ref[...] = z[:, o:o + KV_WIDTH].astype(BF16)
        o += KV_WIDTH
        kw_ref[0, :, :KV_WIDTH] = z[:, o:o + KV_WIDTH].astype(BF16)
        kw_ref[0, :, KV_WIDTH:] = kcw_ref[...]
        o += KV_WIDTH
        vw_ref[0] = z[:, o:o + KV_WIDTH].astype(BF16)
        o += KV_WIDTH
        ga = z[:, o:o + CONV_CH]
        o += CONV_CH
        gb = z[:, o:o + CONV_CH]
        o += CONV_CH
        u_ref[...] = ga * jax.nn.sigmoid(gb)
        gate_ref[...] = jax.nn.sigmoid(z[:, o:o + GATE_PAD])


def _proj(x2, g, w, kconst_slc, kconst_win, *, layer, batch, tm):
    n, d = x2.shape
    seq = n // batch
    tiles = seq // tm
    lead = WINDOW // tm
    tile = lambda bi, j: bi * tiles + jnp.maximum(j - lead, 0)
    row = lambda width: pl.BlockSpec((tm, width), lambda bi, j: (tile(bi, j), 0))
    kconst_s = pl.BlockSpec((tm, LANES), lambda bi, j: (jnp.maximum(j - lead, 0), 0))
    kconst_w = pl.BlockSpec((tm, LANES), lambda bi, j: (j, 0))
    shp = lambda width, dt: jax.ShapeDtypeStruct((n, width), dt)
    padded = lambda width: pl.BlockSpec((1, tm, width), lambda bi, j: (bi, j, 0))
    pshape = lambda width: jax.ShapeDtypeStruct((batch, WINDOW + seq, width), BF16)
    return pl.pallas_call(
        functools.partial(_proj_kernel, lead=lead),
        grid=(batch, tiles + lead),
        in_specs=[row(d), _const_spec((1, d)), _layer_spec(w, layer), kconst_s, kconst_w],
        out_specs=[row(ATTN_WIDTH), row(KV_WIDTH), row(KV_WIDTH), row(KV_WIDTH + LANES), row(KV_WIDTH),
                   padded(KV_WIDTH + LANES), padded(KV_WIDTH), row(CONV_CH), row(GATE_PAD)],
        out_shape=[shp(ATTN_WIDTH, BF16), shp(KV_WIDTH, F32), shp(KV_WIDTH, F32),
                   shp(KV_WIDTH + LANES, BF16), shp(KV_WIDTH, BF16), pshape(KV_WIDTH + LANES), pshape(KV_WIDTH),
                   shp(CONV_CH, F32), shp(GATE_PAD, F32)],
        compiler_params=pltpu.CompilerParams(
            dimension_semantics=("parallel", "arbitrary"), vmem_limit_bytes=VMEM_LIMIT),
        name="proj",
    )(x2, g, w, kconst_slc, kconst_win)


def _cmp_kernel(kf_ref, vf_ref, pk_ref, w1k_ref, w2k_ref, pv_ref, w1v_ref, w2v_ref, ko_ref, vo_ref):
    nchunk = kf_ref.shape[1] // CMP_STRIDE
    row = lax.broadcasted_iota(jnp.int32, (nchunk, 1), 0)
    for f_ref, p_ref, w1_ref, w2_ref, o_ref in ((kf_ref, pk_ref, w1k_ref, w2k_ref, ko_ref),
                                               (vf_ref, pv_ref, w1v_ref, w2v_ref, vo_ref)):
        ha = jnp.zeros((nchunk, N_KV * CMP_HIDDEN), F32)
        hb = jnp.zeros((nchunk, N_KV * CMP_HIDDEN), F32)
        for l in range(CMP_STRIDE):
            tok = f_ref[0, pl.ds(l, nchunk, stride=CMP_STRIDE), :]
            ha = ha + jnp.dot((tok + p_ref[0, l]).astype(BF16), w1_ref[0, l], preferred_element_type=F32)
            hb = hb + jnp.dot((tok + p_ref[1, l]).astype(BF16), w1_ref[1, l], preferred_element_type=F32)
        hid = ha + pltpu.roll(hb, nchunk - 1, 0)
        hid = jnp.where(row < nchunk - 1, hid, 0.0)
        act = (hid * jax.nn.sigmoid(hid)).astype(BF16)
        o_ref[0] = jnp.dot(act, w2_ref[...], preferred_element_type=F32).astype(BF16)


def _compress(kf, vf, pk, w1k, w2k, pv, w1v, w2v, *, layer):
    b, seq, width = kf.shape
    nchunk = seq // CMP_STRIDE
    blk = pl.BlockSpec((1, seq, width), lambda i: (i, 0, 0))
    oblk = pl.BlockSpec((1, nchunk, KV_WIDTH), lambda i: (i, 0, 0))
    wspecs = [_layer_spec(w, layer) for w in (pk, w1k, w2k, pv, w1v, w2v)]
    return pl.pallas_call(
        _cmp_kernel,
        grid=(b,),
        in_specs=[blk, blk] + wspecs,
        out_specs=[oblk, oblk],
        out_shape=[jax.ShapeDtypeStruct((b, nchunk, KV_WIDTH), BF16)] * 2,
        compiler_params=pltpu.CompilerParams(
            dimension_semantics=("parallel",), vmem_limit_bytes=VMEM_LIMIT),
        name="compress",
    )(kf, vf, pk, w1k, w2k, pv, w1v, w2v)


def _attn_kernel(q_ref, kc_ref, vct_ref, ks_ref, vst_ref, kw_ref, vwt_ref, gate_ref, aux_ref, ovt_ref,
                 o_ref, s0_scr, s1_scr, e0_scr, e1_scr, acc_scr, *, kchunk):
    ncmp = kc_ref.shape[1]
    nsel = ovt_ref.shape[0] - ONES_ROWS
    nqb = Q_STEP // Q_BLOCK
    cols = nqb * COLS
    q0 = pl.program_id(1) * Q_STEP
    c0 = pl.program_id(1) * nqb
    nt = (((1,), (1,)), ((), ()))
    qf = q_ref[0].astype(F32)
    col = lax.broadcasted_iota(jnp.int32, (1, cols), 1)
    t_row = q0 + (col // COLS) * Q_BLOCK + (col & (Q_BLOCK - 1))
    zeros_half = jnp.zeros((ROWS, HEAD_DIM), F32)

    def by_group(fn):
        return jnp.concatenate([fn(jb % N_KV, slice(jb * ROWS, (jb + 1) * ROWS)) for jb in range(cols // ROWS)],
                               axis=1)

    qpad, aux0 = [], []
    for qb in range(nqb):
        c_f = (c0 + qb).astype(F32)
        for g in range(N_KV):
            qg = jnp.concatenate(
                [qf[qb * Q_BLOCK:(qb + 1) * Q_BLOCK, (g * GROUP + r) * HEAD_DIM:(g * GROUP + r + 1) * HEAD_DIM]
                 for r in range(GROUP)], axis=0)
            qpad.append(jnp.concatenate([qg, zeros_half] if g == 0 else [zeros_half, qg], axis=1))
            aux0.append(aux_ref[g, 0] + c_f * aux_ref[g, 1])
    qpad = jnp.concatenate(qpad, axis=0)
    aux0 = jnp.concatenate(aux0, axis=0)
    qp0 = jnp.concatenate([qpad, aux0], axis=1).astype(BF16)

    cmp_end = lax.broadcasted_iota(jnp.int32, (ncmp, 1), 0) * CMP_STRIDE + (CMP_BLOCK - 1)
    cmp_valid = cmp_end <= t_row
    has_cmp = t_row >= CMP_BLOCK - 1
    ovt = ovt_ref[...]
    s = lax.dot_general(kc_ref[0], qp0, nt, preferred_element_type=F32)
    s = jnp.where(cmp_valid, s, NEG)
    e = jnp.exp(s - jnp.max(s, axis=0, keepdims=True))
    e_hi = e.astype(BF16)
    e_lo = (e - e_hi.astype(F32)).astype(BF16)
    red = (jnp.dot(ovt, e_hi, preferred_element_type=F32)
           + jnp.dot(ovt, e_lo, preferred_element_type=F32))
    rinv = jnp.where(has_cmp, 1.0 / red[nsel:nsel + 1], 0.0)
    o_cmp = by_group(lambda g, cb: jnp.dot(vct_ref[0, g], e_hi[:, cb], preferred_element_type=F32)) * rinv
    imp_t = red[:nsel] * rinv
    imp = []
    for j in range(nqb * N_KV):
        half = imp_t[:, j * ROWS:j * ROWS + LANES] + imp_t[:, j * ROWS + LANES:(j + 1) * ROWS]
        imp.append(half + pltpu.roll(half, Q_BLOCK, 1))

    sel_lanes = nqb * LANES
    lane = lax.broadcasted_iota(jnp.int32, (1, sel_lanes), 1)
    c_lane = c0 + lane // LANES
    blk = lax.broadcasted_iota(jnp.int32, (nsel, 1), 0)
    forced = (blk == 0) | (blk == c_lane) | (blk == c_lane - 1)
    blk_valid = blk <= c_lane
    lane_g = lax.broadcasted_iota(jnp.int32, (1, LANES), 1) < Q_BLOCK
    score = jnp.concatenate([jnp.where(lane_g, imp[N_KV * qb], imp[N_KV * qb + 1]) for qb in range(nqb)], axis=1)
    score = jnp.where(blk_valid, score + jnp.where(forced, FORCE_BONUS, 0.0), -jnp.inf)
    j_loc = lax.broadcasted_iota(jnp.int32, (SUBLANES, 1), 0)
    nv = nsel // SUBLANES
    tiles = [score[v * SUBLANES:(v + 1) * SUBLANES, :] for v in range(nv)]
    rank = [jnp.zeros((SUBLANES, sel_lanes), F32) for _ in range(nv)]
    for i in range(nsel):
        row = score[i:i + 1, :]
        for v in range(nv):
            if v * SUBLANES + SUBLANES - 1 < i:
                before = jnp.where(row > tiles[v], 1.0, 0.0)
            elif v * SUBLANES > i:
                before = jnp.where(row >= tiles[v], 1.0, 0.0)
            else:
                later = jnp.where(j_loc > i - v * SUBLANES, 1.0, 0.0)
                before = jnp.where(row > tiles[v], 1.0, jnp.where(row == tiles[v], later, 0.0))
            rank[v] = rank[v] + before
    rank = jnp.concatenate(rank, axis=0)
    sel_bias = jnp.where(blk_valid, jnp.where(rank < float(min(N_SEL, nsel)), 0.0, NEG), NEG)
    sel_bias = jnp.concatenate([sel_bias, jnp.zeros((LANES - nsel, sel_lanes), F32)], axis=0).T

    gate_t = gate_ref[0].T

    w_al = pl.multiple_of(q0, LANES)
    kr_top = w_al + lax.broadcasted_iota(jnp.int32, (Q_STEP, 1), 0)
    kr_bot = kr_top + WINDOW
    s = lax.dot_general(kw_ref[0, pl.ds(w_al, WIN_SPAN), :], qp0, nt, preferred_element_type=F32)
    s = jnp.concatenate([jnp.where(kr_top > t_row, s[:Q_STEP], NEG),
                         s[Q_STEP:WINDOW],
                         jnp.where(kr_bot <= t_row + WINDOW, s[WINDOW:], NEG)], axis=0)
    e = jnp.exp(s - jnp.max(s, axis=0, keepdims=True))
    e = e.astype(BF16)
    res = by_group(lambda g, cb: jnp.dot(vwt_ref[0, g, :, pl.ds(w_al, WIN_SPAN)], e[:, cb],
                                         preferred_element_type=F32))
    o_win = res[:HEAD_DIM] * (1.0 / res[HEAD_DIM:HEAD_DIM + 1])

    sel_rows = jnp.concatenate(
        [sel_bias[j * Q_BLOCK:(j + 1) * Q_BLOCK, :] for j in range(nqb * N_KV) for _ in range(GROUP)], axis=0)
    qp = jnp.concatenate([qpad, aux0 + sel_rows], axis=1).astype(BF16)
    s_bufs = (s0_scr, s1_scr)
    e_bufs = (e0_scr, e1_scr)
    n_strip = kchunk // STRIP
    n_full = q0 // kchunk

    def produce(k0, s_buf):
        s = lax.dot_general(ks_ref[0, pl.ds(k0, kchunk), :], qp, nt, preferred_element_type=F32)
        s_buf[...] = s
        return jnp.max(s, axis=0, keepdims=True)

    def soften(s_buf, e_buf, smax, m):
        m_new = jnp.maximum(m, smax)
        for cb in range(cols // COLS):
            cs = slice(cb * COLS, (cb + 1) * COLS)
            m_cb = m_new[:, cs]
            for j in range(n_strip):
                rows = slice(j * STRIP, (j + 1) * STRIP)
                e_buf[rows, cs] = jnp.exp(s_buf[rows, cs] - m_cb).astype(BF16)
        return m_new, jnp.exp(m - m_new)

    def add_values(k0, e_buf, alpha):
        for jb in range(cols // ROWS):
            cb = slice(jb * ROWS, (jb + 1) * ROWS)
            vb = vst_ref[0, jb % N_KV, :, pl.ds(k0, kchunk)]
            acc_scr[:, cb] = alpha[:, cb] * acc_scr[:, cb] + jnp.dot(vb, e_buf[:, cb], preferred_element_type=F32)

    def step(j, carry, cur, nxt):
        smax, m, alpha_prev = carry
        k0 = pl.multiple_of(j * kchunk, kchunk)

        @pl.when(j > 0)
        def _():
            add_values(pl.multiple_of(k0 - kchunk, kchunk), e_bufs[nxt], alpha_prev)

        m, alpha = soften(s_bufs[cur], e_bufs[cur], smax, m)
        return produce(pl.multiple_of(k0 + kchunk, kchunk), s_bufs[nxt]), m, alpha

    def slc_body(j, carry):
        return lax.cond(((n_full - j) & 1) == 0,
                        lambda cr: step(j, cr, 0, 1), lambda cr: step(j, cr, 1, 0), carry)

    acc_scr[...] = jnp.zeros(acc_scr.shape, F32)
    smax0 = lax.cond((n_full & 1) == 0, lambda: produce(0, s0_scr), lambda: produce(0, s1_scr))
    carry0 = (smax0, jnp.full((1, cols), NEG, F32), jnp.ones((1, cols), F32))
    _, m_run, alpha_prev = lax.fori_loop(0, n_full, slc_body, carry0)
    k_last = pl.multiple_of(n_full * kchunk, kchunk)

    @pl.when(n_full > 0)
    def _():
        add_values(pl.multiple_of(k_last - kchunk, kchunk), e1_scr, alpha_prev)

    diag = pl.multiple_of(q0 - k_last, Q_STEP)
    causal = (q0 + lax.broadcasted_iota(jnp.int32, (Q_STEP, 1), 0)) <= t_row
    s0_scr[pl.ds(diag, Q_STEP), :] = jnp.where(causal, s0_scr[pl.ds(diag, Q_STEP), :], NEG)
    mx = s0_scr[0:STRIP, :]
    for j in range(1, n_strip):
        mx = jnp.maximum(mx, s0_scr[j * STRIP:(j + 1) * STRIP, :])
    _, alpha_last = soften(s0_scr, e0_scr, jnp.max(mx, axis=0, keepdims=True), m_run)
    add_values(k_last, e0_scr, alpha_last)
    o_slc = acc_scr[:HEAD_DIM, :] * (1.0 / acc_scr[HEAD_DIM:HEAD_DIM + 1, :])

    def gate_row(br):
        return jnp.concatenate([gate_t[h * N_BRANCH + br:h * N_BRANCH + br + 1, qb * Q_BLOCK:(qb + 1) * Q_BLOCK]
                                for qb in range(nqb) for h in range(N_HEADS)], axis=1)

    og = gate_row(0) * o_cmp + gate_row(1) * o_slc + gate_row(2) * o_win
    og = jnp.concatenate([og, jnp.zeros((LANES - HEAD_DIM, cols), F32)], axis=0).T
    o_ref[0] = jnp.concatenate(
        [jnp.concatenate(
            [og[qb * COLS + h * Q_BLOCK:qb * COLS + (h + 1) * Q_BLOCK, :HEAD_DIM] for h in range(N_HEADS)], axis=1)
         for qb in range(nqb)], axis=0).astype(BF16)


def _attention(q, kc, vct, ks, vst, kw, vwt, gates, aux, ovt, *, kchunk):
    b, seq, _ = q.shape
    cols = Q_STEP // Q_BLOCK * COLS
    qspec = lambda width: pl.BlockSpec((1, Q_STEP, width), lambda i, j: (i, j, 0))
    full = lambda a: pl.BlockSpec((1,) + a.shape[1:], lambda i, j: (i,) + (0,) * (a.ndim - 1))
    return pl.pallas_call(
        functools.partial(_attn_kernel, kchunk=kchunk),
        grid=(b, seq // Q_STEP),
        in_specs=[qspec(ATTN_WIDTH), full(kc), full(vct), full(ks), full(vst), full(kw), full(vwt),
                  qspec(GATE_PAD), _const_spec(aux.shape), _const_spec(ovt.shape)],
        out_specs=qspec(ATTN_WIDTH),
        out_shape=jax.ShapeDtypeStruct((b, seq, ATTN_WIDTH), BF16),
        scratch_shapes=[pltpu.VMEM((kchunk, cols), F32), pltpu.VMEM((kchunk, cols), F32),
                        pltpu.VMEM((kchunk, cols), BF16), pltpu.VMEM((kchunk, cols), BF16),
                        pltpu.VMEM((VG_ROWS, cols), F32)],
        compiler_params=pltpu.CompilerParams(
            dimension_semantics=("parallel", "arbitrary"), vmem_limit_bytes=VMEM_LIMIT),
        name="nsa_attention",
    )(q, kc, vct, ks, vst, kw, vwt, gates, aux, ovt)


def _attn_tables(seq):
    nsel = seq // SEL_BLOCK
    ncmp = seq // CMP_STRIDE
    lane = np.arange(LANES)[None, :]

    def key_consts(pos, is_pad):
        blk = (pos // SEL_BLOCK)[:, None]
        off = (pos % SEL_BLOCK)[:, None]
        live = (~is_pad)[:, None]
        t = np.where((lane == blk) & (lane < SEL_BLOCK), 1.0, 0.0)
        t = t + np.where(lane == AUX_BLK, blk, 0) + np.where(lane == AUX_OFF, off, 0)
        t = t + np.where((lane == AUX_ONE_C) | (lane == AUX_ONE_Q), 1.0, 0.0)
        return np.where(live, t, np.where(lane == AUX_PAD, 1.0, 0.0)).astype(np.float32)

    pos = np.arange(seq)
    k_slc = key_consts(pos, pos < 0)
    posw = np.arange(seq + WINDOW) - WINDOW
    k_win = key_consts(np.maximum(posw, 0), posw < 0)
    k_win = np.where(lane < SEL_BLOCK, 0.0, k_win).astype(np.float32)
    n = np.arange(ncmp)[:, None]
    k_cmp = (np.where((lane == AUX_ONE_C) | (lane == AUX_ONE_Q) | (lane == AUX_CMP_ONE), 1.0, 0.0)
             + np.where(lane == AUX_CMP_N, n, 0)).astype(np.float32)

    row = np.arange(ROWS)
    qi = (row % Q_BLOCK).astype(np.float32)[:, None]
    aux = []
    for g in range(N_KV):
        slope = np.exp2(-(g * GROUP + row // Q_BLOCK + 1).astype(np.float32))[:, None]
        t0 = (np.where(lane == AUX_BLK, slope * SEL_BLOCK, 0.0) + np.where(lane == AUX_OFF, slope, 0.0)
              + np.where(lane == AUX_ONE_Q, -slope * qi, 0.0)
              + np.where(lane == AUX_CMP_N, slope * CMP_STRIDE, 0.0)
              + np.where(lane == AUX_CMP_ONE, slope * (CMP_BLOCK - 1), 0.0)
              + np.where(lane == AUX_PAD, NEG, 0.0))
        t1 = np.where(lane == AUX_ONE_C, -slope * Q_BLOCK, 0.0)
        aux.append(np.stack([t0, t1]))
    aux = np.stack(aux).astype(np.float32)

    cmp_start = np.arange(ncmp) * CMP_STRIDE
    sel_start = np.arange(nsel) * SEL_BLOCK
    ovt = np.clip(np.minimum(cmp_start[None] + CMP_BLOCK, sel_start[:, None] + SEL_BLOCK)
                  - np.maximum(cmp_start[None], sel_start[:, None]), 0, None).astype(np.float32) / CMP_BLOCK
    ovt = np.concatenate([ovt, np.ones((ONES_ROWS, ncmp), np.float32)], axis=0)
    bf = lambda a: jnp.asarray(a, BF16)
    return bf(k_slc), bf(k_win), bf(k_cmp), jnp.asarray(aux), bf(ovt)


def _mix_out_kernel(x_ref, attn_ref, u_ref, halo_ref, cw_ref, cb_ref, lg_ref, lb_ref, wo_ref, o_ref,
                    ush_ref, cv_ref, *, rows_per_step):
    ts = u_ref.shape[1]
    span = HALO + ts
    i = pl.program_id(1)
    ush_ref[0, 0:HALO, :] = jnp.where(i > 0, halo_ref[0], 0.0)
    ush_ref[0, HALO:span, :] = u_ref[0]
    for b in range(1, SUBLANES):
        ush_ref[b, 0:span - SUBLANES, :] = ush_ref[0, b:span - SUBLANES + b, :]
    cw = cw_ref[...]
    base = HALO - (CONV_WIDTH - 1)
    for rc in range(ts // rows_per_step):
        r0 = rc * rows_per_step
        acc = jnp.zeros((rows_per_step, CONV_CH), F32) + cb_ref[...]
        for k in range(CONV_WIDTH):
            shift, start = (base + k) % SUBLANES, (base + k) // SUBLANES * SUBLANES
            acc = acc + cw[k:k + 1, :] * ush_ref[shift, r0 + start:r0 + start + rows_per_step, :]
        mu = jnp.mean(acc, axis=-1, keepdims=True)
        cen = acc - mu
        var = jnp.mean(cen * cen, axis=-1, keepdims=True)
        y = cen * lax.rsqrt(var + EPS) * lg_ref[...] + lb_ref[...]
        cv_ref[r0:r0 + rows_per_step, :] = (y * jax.nn.sigmoid(y)).astype(BF16)
    heads = jnp.concatenate([attn_ref[0], cv_ref[...]], axis=1)
    o_ref[0] = x_ref[0] + jnp.dot(heads, wo_ref[...], preferred_element_type=F32)


def _mix_out(x, attn, u, cw, cb, lg, lb, wo, *, layer, ts):
    b, seq, d = x.shape
    blk = lambda width: pl.BlockSpec((1, ts, width), lambda bi, i: (bi, i, 0))
    halo = pl.BlockSpec((1, HALO, CONV_CH), lambda bi, i: (bi, jnp.maximum(i * (ts // HALO) - 1, 0), 0))
    return pl.pallas_call(
        functools.partial(_mix_out_kernel, rows_per_step=32),
        grid=(b, seq // ts),
        in_specs=[blk(d), blk(ATTN_WIDTH), blk(CONV_CH), halo,
                  _const_spec(cw.shape), _const_spec(cb.shape), _const_spec(lg.shape), _const_spec(lb.shape),
                  _layer_spec(wo, layer)],
        out_specs=blk(d),
        out_shape=jax.ShapeDtypeStruct((b, seq, d), F32),
        scratch_shapes=[pltpu.VMEM((SUBLANES, HALO + ts, CONV_CH), F32), pltpu.VMEM((ts, CONV_CH), BF16)],
        compiler_params=pltpu.CompilerParams(
            dimension_semantics=("parallel", "arbitrary"), vmem_limit_bytes=VMEM_LIMIT),
        name="conv_outproj",
    )(x, attn, u, u, cw, cb, lg, lb, wo)


def _prep_w_in(w_in):
    o_gl = ATTN_WIDTH + 6 * KV_WIDTH
    n_gl = N_HEADS * N_BRANCH
    head = w_in[..., :o_gl]
    gl = w_in[..., o_gl:o_gl + n_gl]
    rest = w_in[..., o_gl + n_gl:]
    gl = jnp.pad(gl, ((0, 0), (0, 0), (0, GATE_PAD - n_gl)))
    return jnp.concatenate([head, rest, gl], axis=-1).astype(BF16)


def _prep_cmp(pos, w1, w2):
    nl = pos.shape[0]
    half = CMP_STRIDE
    eye = jnp.eye(N_KV, dtype=w1.dtype)
    w1r = w1.reshape(nl, 2, half, HEAD_DIM, CMP_HIDDEN)
    w1x = w1r[:, :, :, None, :, None, :] * eye[:, None, :, None]
    w1x = w1x.reshape(nl, 2, half, N_KV * HEAD_DIM, N_KV * CMP_HIDDEN).astype(BF16)
    w2x = w2[:, None, :, None, :] * eye[:, None, :, None]
    w2x = w2x.reshape(nl, N_KV * CMP_HIDDEN, N_KV * HEAD_DIM).astype(BF16)
    pr = pos.reshape(nl, 2, half, 1, 1, HEAD_DIM)
    px = jnp.broadcast_to(pr, (nl, 2, half, 1, N_KV, HEAD_DIM)).reshape(nl, 2, half, 1, N_KV * HEAD_DIM)
    return px, w1x, w2x


def kernel(x, ffn1_norm, ffn1_w_gate, ffn1_w_up, ffn1_w_down, mix_norm, w_in, cmp_pos_k, cmp_k_w1, cmp_k_w2,
           cmp_pos_v, cmp_v_w1, cmp_v_w2, conv_w, conv_b, conv_norm_g, conv_norm_b, w_out, ffn2_norm,
           ffn2_w_gate, ffn2_w_up, ffn2_w_down, final_norm):
    b, seq, d = x.shape
    depth = ffn1_norm.shape[0]
    n = b * seq
    tm = 512
    ts = 512
    kchunk = 512
    assert d == D_MODEL and seq % kchunk == 0 and seq % tm == 0 and seq // SEL_BLOCK <= SEL_BLOCK
    assert WINDOW % tm == 0 and kchunk % Q_STEP == 0 and Q_STEP % LANES == 0 and Q_STEP < WINDOW

    bf = lambda w: w.astype(BF16)
    w1g, w1u, w1d = bf(ffn1_w_gate), bf(ffn1_w_up), bf(ffn1_w_down)
    w2g, w2u, w2d = bf(ffn2_w_gate), bf(ffn2_w_up), bf(ffn2_w_down)
    w_in_p = _prep_w_in(w_in)
    pk, w1k, w2k = _prep_cmp(cmp_pos_k, cmp_k_w1, cmp_k_w2)
    pv, w1v, w2v = _prep_cmp(cmp_pos_v, cmp_v_w1, cmp_v_w2)
    w_o = bf(w_out)
    row = lambda v: v.reshape(1, -1)
    fg = row(final_norm)
    kconst_slc, kconst_win, kconst_cmp, aux, ovt = _attn_tables(seq)

    def with_consts(k, consts):
        return jnp.concatenate([k, jnp.broadcast_to(consts, (b,) + consts.shape)], axis=-1)

    x2 = x.reshape(n, d)
    r3 = lambda a: a.reshape(b, seq, a.shape[-1])
    tr = lambda a: jnp.swapaxes(a, 1, 2).reshape(b, N_KV, HEAD_DIM, a.shape[1])
    tr1 = lambda a: jnp.pad(tr(a), ((0, 0), (0, 0), (0, ONES_ROWS), (0, 0)), constant_values=1)
    for l in range(depth):
        x2 = _ffn(x2, row(ffn1_norm[l]), w1g, w1u, w1d, fg, layer=l, final=False, tm=tm)
        q, kc, vc, ks, vs, kw, vw, u, gates = _proj(x2, row(mix_norm[l]), w_in_p, kconst_slc, kconst_win,
                                                    layer=l, batch=b, tm=tm)
        kcc, vcc = _compress(r3(kc), r3(vc), pk, w1k, w2k, pv, w1v, w2v, layer=l)
        attn = _attention(r3(q), with_consts(kcc, kconst_cmp), tr(vcc),
                          r3(ks), tr1(r3(vs)), kw, tr1(vw),
                          r3(gates), aux, ovt, kchunk=kchunk)
        x3 = _mix_out(x2.reshape(b, seq, d), attn, r3(u), conv_w[l], row(conv_b[l]), row(conv_norm_g[l]),
                      row(conv_norm_b[l]), w_o, layer=l, ts=ts)
        x2 = _ffn(x3.reshape(n, d), row(ffn2_norm[l]), w2g, w2u, w2d, fg, layer=l,
                  final=(l == depth - 1), tm=tm)
    return x2.reshape(b, seq, d)
```
